```python
import jax
import jax.numpy as jnp
from jax import lax

D_MODEL = 1024
BATCH = 32
SEQ = 2048
DEPTH = 1
DEC_BATCH = 32
DEC_SEQ = 64
PAST_LEN = 1024

CHUNK = 64
N_HEADS = 16
HEAD_DIM = 64
D_ATTN = N_HEADS * HEAD_DIM
D_RNN = D_MODEL
RNN_BLOCKS = 16
RNN_BLOCK = D_RNN // RNN_BLOCKS
CONV_W = 4
RG_C = 8.0
D_FF = 4 * D_MODEL
Q_BLOCK = 128
EPS = 1e-6
_SPLITS = (D_ATTN, 2 * D_ATTN, 3 * D_ATTN, 3 * D_ATTN + N_HEADS,
           3 * D_ATTN + N_HEADS + D_RNN, 3 * D_ATTN + N_HEADS + 2 * D_RNN,
           3 * D_ATTN + N_HEADS + 2 * D_RNN + D_MODEL)
D_IN = 3 * D_ATTN + N_HEADS + 2 * D_RNN + 2 * D_MODEL

kernel_name = 'fox_rglru_gated_parallel_streaming_step'


def _rmsnorm(x, g):
    xf = x.astype(jnp.float32)
    xf = xf * lax.rsqrt(jnp.mean(xf * xf, axis=-1, keepdims=True) + EPS)
    return (xf * g.astype(jnp.float32)).astype(x.dtype)


def _fox_attention(q, k, v, c_q, c_k, q_pos, k_pos):
    b, t, h, dh = q.shape
    qb = min(t, Q_BLOCK)
    nb = t // qb
    scale = dh ** -0.5
    ck = jnp.transpose(c_k, (0, 2, 1))[:, :, None, :]

    def block(args):
        q_b, cq_b, pos_b = args
        s = jnp.einsum('bqhd,bkhd->bhqk', q_b, k, preferred_element_type=jnp.float32) * scale
        s = s + jnp.transpose(cq_b, (0, 2, 1))[..., None] - ck
        mask = k_pos[None, :] <= pos_b[:, None]
        s = jnp.where(mask[None, None], s, -jnp.inf)
        p = jax.nn.softmax(s, axis=-1)
        return jnp.einsum('bhqk,bkhd->bqhd', p.astype(v.dtype), v)

    q_blocks = jnp.transpose(q.reshape(b, nb, qb, h, dh), (1, 0, 2, 3, 4))
    cq_blocks = jnp.transpose(c_q.reshape(b, nb, qb, h), (1, 0, 2, 3))
    pos_blocks = q_pos.reshape(nb, qb)
    o = lax.map(block, (q_blocks, cq_blocks, pos_blocks))
    return jnp.transpose(o, (1, 0, 2, 3, 4)).reshape(b, t, h * dh)


def _linear_scan(a, bx, h0):
    def step(hc, ab):
        a_t, b_t = ab
        hc = a_t * hc + b_t
        return hc, hc
    h_last, hs = lax.scan(step, h0, (jnp.transpose(a, (1, 0, 2)), jnp.transpose(bx, (1, 0, 2))))
    return jnp.transpose(hs, (1, 0, 2)), h_last


def _layer(x, k_hist, v_hist, logf_hist, conv_hist, h0,
           norm_mix_g, w_in, b_f, conv_w, conv_b, w_rg_a, b_rg_a, w_rg_x, b_rg_x, rg_lambda,
           w_proj_attn, w_proj_rnn, w_out, norm_mlp_g, w_up, w_down):
    bsz, t, _ = x.shape
    past = k_hist.shape[1]
    hn = _rmsnorm(x, norm_mix_g)
    z = hn @ w_in
    q, k, v, f_logit, xr, yr, ga, gb = jnp.split(z, _SPLITS, axis=-1)

    q = q.reshape(bsz, t, N_HEADS, HEAD_DIM)
    k = k.reshape(bsz, t, N_HEADS, HEAD_DIM)
    v = v.reshape(bsz, t, N_HEADS, HEAD_DIM)
    logf = jax.nn.log_sigmoid(f_logit.astype(jnp.float32) + b_f.astype(jnp.float32))
    k_all = jnp.concatenate([k_hist.astype(k.dtype), k], axis=1)
    v_all = jnp.concatenate([v_hist.astype(v.dtype), v], axis=1)
    logf_all = jnp.concatenate([logf_hist.astype(jnp.float32), logf], axis=1)
    c_all = jnp.cumsum(logf_all, axis=1)
    k_pos = jnp.arange(past + t)
    q_pos = past + jnp.arange(t)
    o_attn = _fox_attention(q, k_all, v_all, c_all[:, past:], c_all, q_pos, k_pos)

    conv_in = jnp.concatenate([conv_hist.astype(xr.dtype), xr], axis=1)
    xc = conv_b + sum(conv_in[:, w:w + t] * conv_w[w] for w in range(CONV_W))
    new_conv = conv_in[:, t:]
    xb = xc.reshape(bsz, t, RNN_BLOCKS, RNN_BLOCK)
    r = jax.nn.sigmoid(jnp.einsum('btnd,nde->btne', xb, w_rg_a).reshape(bsz, t, D_RNN) + b_rg_a)
    i = jax.nn.sigmoid(jnp.einsum('btnd,nde->btne', xb, w_rg_x).reshape(bsz, t, D_RNN) + b_rg_x)
    log_a = -RG_C * r.astype(jnp.float32) * jax.nn.softplus(-rg_lambda.astype(jnp.float32))
    a = jnp.exp(log_a)
    bx = jnp.sqrt(-jnp.expm1(2.0 * log_a)) * (i * xc).astype(jnp.float32)
    hs, h_last = _linear_scan(a, bx, h0.astype(jnp.float32))
    o_rnn = hs.astype(x.dtype) * jax.nn.gelu(yr)

    y_a = o_attn @ w_proj_attn
    y_b = o_rnn @ w_proj_rnn
    merged = jax.nn.sigmoid(ga) * y_a + jax.nn.sigmoid(gb) * y_b
    x = x + merged @ w_out

    h2 = _rmsnorm(x, norm_mlp_g)
    x = x + jnp.square(jax.nn.relu(h2 @ w_up)) @ w_down
    return x, k, v, logf, new_conv, h_last


def setup_inputs(seed: int = 0) -> dict:
    key = jax.random.key(seed)
    ks = jax.random.split(key, 26)
    f32 = jnp.float32

    def nrm(k, shape, scale):
        return jax.random.normal(k, shape, f32) * scale

    u = jax.random.uniform(ks[16], (DEPTH, D_RNN), f32, 0.9, 0.999)
    a0 = u ** (1.0 / RG_C)
    return {
        'x_prompt': nrm(ks[0], (BATCH, SEQ, D_MODEL), 1.0),
        'x_sample': nrm(ks[1], (DEC_BATCH, DEC_SEQ, D_MODEL), 1.0),
        'cache_k': nrm(ks[2], (DEPTH, DEC_BATCH, PAST_LEN, N_HEADS, HEAD_DIM), 1.0),
        'cache_v': nrm(ks[3], (DEPTH, DEC_BATCH, PAST_LEN, N_HEADS, HEAD_DIM), 1.0),
        'cache_logf': jax.nn.log_sigmoid(2.5 + nrm(ks[4], (DEPTH, DEC_BATCH, PAST_LEN, N_HEADS), 1.0)),
        'state_conv': nrm(ks[5], (DEPTH, DEC_BATCH, CONV_W - 1, D_RNN), 1.0),
        'state_rglru': nrm(ks[6], (DEPTH, DEC_BATCH, D_RNN), 0.5),
        'norm_mix_g': 1.0 + nrm(ks[7], (DEPTH, D_MODEL), 0.05),
        'w_in': nrm(ks[8], (DEPTH, D_MODEL, D_IN), D_MODEL ** -0.5),
        'b_f': jax.random.uniform(ks[9], (DEPTH, N_HEADS), f32, 1.0, 4.0),
        'conv_w': nrm(ks[10], (DEPTH, CONV_W, D_RNN), CONV_W ** -0.5),
        'conv_b': nrm(ks[11], (DEPTH, D_RNN), 0.01),
        'w_rg_a': nrm(ks[12], (DEPTH, RNN_BLOCKS, RNN_BLOCK, RNN_BLOCK), RNN_BLOCK ** -0.5),
        'b_rg_a': nrm(ks[13], (DEPTH, D_RNN), 0.01),
        'w_rg_x': nrm(ks[14], (DEPTH, RNN_BLOCKS, RNN_BLOCK, RNN_BLOCK), RNN_BLOCK ** -0.5),
        'b_rg_x': nrm(ks[15], (DEPTH, D_RNN), 0.01),
        'rg_lambda': jnp.log(a0) - jnp.log1p(-a0),
        'w_proj_attn': nrm(ks[17], (DEPTH, D_ATTN, D_MODEL), D_ATTN ** -0.5),
        'w_proj_rnn': nrm(ks[18], (DEPTH, D_RNN, D_MODEL), D_RNN ** -0.5),
        'w_out': nrm(ks[19], (DEPTH, D_MODEL, D_MODEL), D_MODEL ** -0.5),
        'norm_mlp_g': 1.0 + nrm(ks[20], (DEPTH, D_MODEL), 0.05),
        'w_up': nrm(ks[21], (DEPTH, D_MODEL, D_FF), D_MODEL ** -0.5),
        'w_down': nrm(ks[22], (DEPTH, D_FF, D_MODEL), D_FF ** -0.5),
        'norm_final_g': 1.0 + nrm(ks[23], (D_MODEL,), 0.05),
    }


def reference(x_prompt, x_sample, cache_k, cache_v, cache_logf, state_conv, state_rglru,
              norm_mix_g, w_in, b_f, conv_w, conv_b, w_rg_a, b_rg_a, w_rg_x, b_rg_x, rg_lambda,
              w_proj_attn, w_proj_rnn, w_out, norm_mlp_g, w_up, w_down, norm_final_g):
    bp = x_prompt.shape[0]
    xp = x_prompt
    xs = x_sample
    kp, vp, lp, cp, hp = [], [], [], [], []
    ksm, vsm, lsm, csm, hsm = [], [], [], [], []
    for l in range(DEPTH):
        weights = (norm_mix_g[l], w_in[l], b_f[l], conv_w[l], conv_b[l], w_rg_a[l], b_rg_a[l],
                   w_rg_x[l], b_rg_x[l], rg_lambda[l], w_proj_attn[l], w_proj_rnn[l], w_out[l],
                   norm_mlp_g[l], w_up[l], w_down[l])
        xp, k1, v1, f1, c1, h1 = _layer(
            xp,
            jnp.zeros((bp, 0, N_HEADS, HEAD_DIM), xp.dtype),
            jnp.zeros((bp, 0, N_HEADS, HEAD_DIM), xp.dtype),
            jnp.zeros((bp, 0, N_HEADS), jnp.float32),
            jnp.zeros((bp, CONV_W - 1, D_RNN), xp.dtype),
            jnp.zeros((bp, D_RNN), jnp.float32),
            *weights)
        xs, k2, v2, f2, c2, h2 = _layer(
            xs, cache_k[l], cache_v[l], cache_logf[l], state_conv[l], state_rglru[l], *weights)
        kp.append(k1); vp.append(v1); lp.append(f1); cp.append(c1); hp.append(h1)
        ksm.append(k2); vsm.append(v2); lsm.append(f2); csm.append(c2); hsm.append(h2)
    y_prompt = _rmsnorm(xp, norm_final_g)
    y_sample = _rmsnorm(xs, norm_final_g)
    return (y_prompt, y_sample,
            jnp.stack(kp), jnp.stack(vp), jnp.stack(lp), jnp.stack(cp), jnp.stack(hp),
            jnp.stack(ksm), jnp.stack(vsm), jnp.stack(lsm), jnp.stack(csm), jnp.stack(hsm))
```

```python
import functools

import jax
import jax.numpy as jnp
from jax import lax
from jax.experimental import pallas as pl
from jax.experimental.pallas import tpu as pltpu

F32 = jnp.float32
BF16 = jnp.bfloat16

N_HEADS = 16
HEAD_DIM = 64
CONV_W = 4
RNN_BLOCK = 64
RG_C = 8.0
EPS = 1e-6

V7X_LANES = 128
V7X_SUBLANES = 8
V7X_MXU_DIM = 256
V7X_VMEM_BYTES = 64 * 1024 * 1024

HEADS_PER_STEP = V7X_LANES // HEAD_DIM
GATE_BLOCKS_PER_TILE = V7X_MXU_DIM // RNN_BLOCK
N_PROJ = 7
TOKEN_TILE = 512
ATTN_Q_TILE = 512
ATTN_K_TILE = 512
RNN_TIME_TILE = 256
CUMSUM_BLOCK = 256


def _vmem_limit(estimate_bytes):
    return int(min(estimate_bytes + (12 << 20), V7X_VMEM_BYTES - (6 << 20)))


def _params(semantics, vmem_estimate):
    return pltpu.CompilerParams(dimension_semantics=semantics,
                                vmem_limit_bytes=_vmem_limit(vmem_estimate))


def _const_spec(shape):
    zeros = (0,) * len(shape)
    return pl.BlockSpec(shape, lambda *_: zeros, pipeline_mode=pl.Buffered(1))


def _rmsnorm(x, g):
    x = x * lax.rsqrt(jnp.mean(x * x, axis=-1, keepdims=True) + EPS)
    return x * g


def _log_sigmoid(x):
    return jnp.minimum(x, 0.0) - jnp.log1p(jnp.exp(-jnp.abs(x)))


def _softplus(x):
    return jnp.maximum(x, 0.0) + jnp.log1p(jnp.exp(-jnp.abs(x)))


def _gelu_tanh(x):
    c = 0.7978845608028654
    return x * (0.5 * (1.0 + jnp.tanh(c * (x + 0.044715 * (x * x * x)))))


def _split3(x):
    hi = x.astype(BF16)
    r1 = x - hi.astype(F32)
    mid = r1.astype(BF16)
    lo = (r1 - mid.astype(F32)).astype(BF16)
    return hi, mid, lo


def _in_proj_kernel(x_ref, g_ref, w_ref, wf_ref, bf_ref,
                    q_ref, k_ref, v_ref, xr_ref, yr_ref, ga_ref, gb_ref, lf_ref, lft_ref, tail_ref,
                    *, seg_len):
    hn = _rmsnorm(x_ref[...], g_ref[...]).astype(BF16)
    tm = hn.shape[0]
    outs = (q_ref, k_ref, v_ref, xr_ref, yr_ref, ga_ref, gb_ref)
    for s, o_ref in enumerate(outs):
        r = jnp.dot(hn, w_ref[s], preferred_element_type=F32)
        o_ref[...] = r.astype(o_ref.dtype)
        if o_ref is xr_ref:
            for j in range(tm // seg_len):
                end = (j + 1) * seg_len
                tail_ref[j] = r[end - V7X_SUBLANES:end, :]
    zf = jnp.dot(hn, wf_ref[...], preferred_element_type=F32)
    lf = _log_sigmoid(zf + bf_ref[...])
    lf_ref[...] = lf[:, :N_HEADS]
    lft_ref[...] = lf.T[:N_HEADS, :]


def _in_proj(x2d, g, w_cat, wf, bf, seq_len):
    n, d = x2d.shape
    tm = min(TOKEN_TILE, n)
    assert n % tm == 0 and (seq_len % tm == 0 or tm % seq_len == 0)
    seg_len = min(seq_len, tm)
    nseg = tm // seg_len
    tiles_per_stream = max(1, seq_len // tm)
    n_streams = n // seq_len
    row = lambda i: (i, 0)
    tok_bf16 = jax.ShapeDtypeStruct((n, d), BF16)
    tok_f32 = jax.ShapeDtypeStruct((n, d), F32)
    out_shape = (tok_bf16, tok_f32, tok_f32, tok_bf16, tok_bf16, tok_bf16, tok_bf16,
                 jax.ShapeDtypeStruct((n, N_HEADS), F32),
                 jax.ShapeDtypeStruct((N_HEADS, n), F32),
                 jax.ShapeDtypeStruct((n_streams, V7X_SUBLANES, d), F32))
    tok_spec = pl.BlockSpec((tm, d), row)
    out_specs = (tok_spec,) * N_PROJ + (
        pl.BlockSpec((tm, N_HEADS), row),
        pl.BlockSpec((N_HEADS, tm), lambda i: (0, i)),
        pl.BlockSpec((nseg, V7X_SUBLANES, d), lambda i: (i // tiles_per_stream, 0, 0)))
    vmem = (w_cat.size + wf.size) * 2 + 2 * tm * d * 4 + 2 * tm * d * (2 * 4 + 5 * 2) + 6 * tm * d * 4
    return pl.pallas_call(
        functools.partial(_in_proj_kernel, seg_len=seg_len),
        grid=(n // tm,),
        in_specs=[tok_spec, _const_spec(g.shape), _const_spec(w_cat.shape),
                  _const_spec(wf.shape), _const_spec(bf.shape)],
        out_specs=out_specs,
        out_shape=out_shape,
        compiler_params=_params(("arbitrary",), vmem),
        name="in_proj",
    )(x2d, g, w_cat, wf, bf)


def _cumsum_kernel(lf_ref, lft_ref, c_ref, ct_ref):
    tk = lf_ref.shape[0]
    blk = min(CUMSUM_BLOCK, tk)
    row = lax.broadcasted_iota(jnp.int32, (blk, blk), 0)
    col = lax.broadcasted_iota(jnp.int32, (blk, blk), 1)
    lower = (row >= col).astype(BF16)
    upper = (row <= col).astype(BF16)
    carry = jnp.zeros((1, N_HEADS), F32)
    carry_t = jnp.zeros((N_HEADS, 1), F32)
    for s in range(0, tk, blk):
        sz = min(blk, tk - s)
        c = carry
        for part in _split3(lf_ref[s:s + sz, :]):
            c = c + jnp.dot(lower[:sz, :sz], part, preferred_element_type=F32)
        c_ref[s:s + sz, :] = c
        carry = c[sz - 1:sz, :]
        ct = carry_t
        for part in _split3(lft_ref[:, s:s + sz]):
            ct = ct + jnp.dot(part, upper[:sz, :sz], preferred_element_type=F32)
        ct_ref[:, s:s + sz] = ct
        carry_t = ct[:, sz - 1:sz]


def _cumsum(lf, lft):
    b, tk, h = lf.shape
    return pl.pallas_call(
        _cumsum_kernel,
        grid=(b,),
        in_specs=[pl.BlockSpec((None, tk, h), lambda i: (i, 0, 0)),
                  pl.BlockSpec((None, h, tk), lambda i: (i, 0, 0))],
        out_specs=(pl.BlockSpec((None, tk, h), lambda i: (i, 0, 0)),
                   pl.BlockSpec((None, h, tk), lambda i: (i, 0, 0))),
        out_shape=(jax.ShapeDtypeStruct((b, tk, h), F32), jax.ShapeDtypeStruct((b, h, tk), F32)),
        compiler_params=_params(("arbitrary",), 8 * tk * V7X_LANES * 4),
        name="logf_cumsum",
    )(lf, lft)


def _attn_kernel(*refs, tq, tkf, past, has_hist):
    if has_hist:
        q_ref, kn_ref, vn_ref, kh_ref, vh_ref, cq_ref, ckd_ref, ckf_ref, o_ref, k_sc, v_sc = refs
    else:
        q_ref, kn_ref, vn_ref, cq_ref, ckd_ref, ckf_ref, o_ref, k_sc, v_sc = refs
    qi = pl.program_id(2)
    t_new = kn_ref.shape[0]

    @pl.when(qi == 0)
    def _stage_keys():
        if has_hist:
            k_sc[0:past, :] = kh_ref[...].astype(BF16)
            v_sc[0:past, :] = vh_ref[...].astype(BF16)
        k_sc[past:past + t_new, :] = kn_ref[...].astype(BF16)
        v_sc[past:past + t_new, :] = vn_ref[...].astype(BF16)

    q = q_ref[...]
    lane = lax.broadcasted_iota(jnp.int32, q.shape, 1)
    first = lane < HEAD_DIM
    zero = jnp.zeros_like(q)
    q_heads = (jnp.where(first, q, zero), jnp.where(first, zero, q))
    cq = cq_ref[...]
    cq_heads = (cq[:, 0:1], cq[:, 1:2])

    def scores(qh, keys, cqh, ckh):
        s = lax.dot_general(qh, keys, (((1,), (1,)), ((), ())), preferred_element_type=F32)
        return (s + cqh) - ckh

    off = pl.multiple_of(past + qi * tq, tq)
    kd = k_sc[pl.ds(off, tq), :]
    vd = v_sc[pl.ds(off, tq), :]
    ckd = ckd_ref[...]
    causal = (lax.broadcasted_iota(jnp.int32, (tq, tq), 1)
              <= lax.broadcasted_iota(jnp.int32, (tq, tq), 0))
    carry = []
    for h in range(HEADS_PER_STEP):
        s = scores(q_heads[h], kd, cq_heads[h], ckd[h:h + 1, :])
        s = jnp.where(causal, s, -jnp.inf)
        m = jnp.max(s, axis=1, keepdims=True)
        p = jnp.exp(s - m)
        l = jnp.sum(p, axis=1, keepdims=True)
        acc = jnp.dot(p.astype(BF16), vd, preferred_element_type=F32)
        carry += [m, l, acc]

    def full_chunk(j, carry):
        start = pl.multiple_of(j * tkf, tkf)
        ks = k_sc[pl.ds(start, tkf), :]
        vs = v_sc[pl.ds(start, tkf), :]
        ck = ckf_ref[j]
        out = []
        for h in range(HEADS_PER_STEP):
            m, l, acc = carry[3 * h:3 * h + 3]
            s = scores(q_heads[h], ks, cq_heads[h], ck[h:h + 1, :])
            m_new = jnp.maximum(m, jnp.max(s, axis=1, keepdims=True))
            alpha = jnp.exp(m - m_new)
            p = jnp.exp(s - m_new)
            l = alpha * l + jnp.sum(p, axis=1, keepdims=True)
            acc = alpha * acc + jnp.dot(p.astype(BF16), vs, preferred_element_type=F32)
            out += [m_new, l, acc]
        return tuple(out)

    n_full = (past + qi * tq) // tkf if not has_hist else past // tkf
    carry = lax.fori_loop(0, n_full, full_chunk, tuple(carry))
    o_a = carry[2] / carry[1]
    o_b = carry[5] / carry[4]
    o_ref[...] = jnp.where(first, o_a, o_b).astype(o_ref.dtype)


def _attention(q, k_new, v_new, k_hist, v_hist, c, ct):
    b, t, d = q.shape
    has_hist = k_hist is not None
    past = k_hist.shape[1] if has_hist else 0
    tk_all = past + t
    tq = min(ATTN_Q_TILE, t)
    tkf = ATTN_K_TILE
    nq = t // tq
    n_pairs = N_HEADS // HEADS_PER_STEP
    assert t % tq == 0 and ((has_hist and nq == 1 and past % tkf == 0) or (not has_hist and tq == tkf))
    nfc = past // tkf if has_hist else nq
    cq = c[:, past:, :].reshape(b, t, n_pairs, HEADS_PER_STEP).transpose(0, 2, 1, 3)
    ct4 = ct.reshape(b, n_pairs, HEADS_PER_STEP, tk_all)
    ckd = ct4[..., past:].reshape(b, n_pairs, HEADS_PER_STEP, nq, tq).transpose(0, 1, 3, 2, 4)
    ckf = ct4[..., :nfc * tkf].reshape(b, n_pairs, HEADS_PER_STEP, nfc, tkf).transpose(0, 1, 3, 2, 4)

    lanes = V7X_LANES
    new_spec = pl.BlockSpec((None, t, lanes), lambda i, p, j: (i, 0, p))
    in_specs = [pl.BlockSpec((None, tq, lanes), lambda i, p, j: (i, j, p)), new_spec, new_spec]
    args = [q, k_new, v_new]
    if has_hist:
        hist_spec = pl.BlockSpec((None, past, lanes), lambda i, p, j: (i, 0, p))
        in_specs += [hist_spec, hist_spec]
        args += [k_hist, v_hist]
    in_specs += [
        pl.BlockSpec((None, None, tq, HEADS_PER_STEP), lambda i, p, j: (i, p, j, 0)),
        pl.BlockSpec((None, None, None, HEADS_PER_STEP, tq), lambda i, p, j: (i, p, j, 0, 0)),
        pl.BlockSpec((None, None, nfc, HEADS_PER_STEP, tkf), lambda i, p, j: (i, p, 0, 0, 0)),
    ]
    args += [cq, ckd, ckf]
    vmem = (4 * tk_all * lanes * 4 + 2 * tk_all * lanes * 2 + 12 * tq * max(tq, tkf) * 4
            + 4 * tq * lanes * 4 + 2 * tq * lanes * 4)
    return pl.pallas_call(
        functools.partial(_attn_kernel, tq=tq, tkf=tkf, past=past, has_hist=has_hist),
        grid=(b, n_pairs, nq),
        in_specs=in_specs,
        out_specs=pl.BlockSpec((None, tq, lanes), lambda i, p, j: (i, j, p)),
        out_shape=jax.ShapeDtypeStruct((b, t, d), BF16),
        scratch_shapes=[pltpu.VMEM((tk_all, lanes), BF16), pltpu.VMEM((tk_all, lanes), BF16)],
        compiler_params=_params(("arbitrary", "arbitrary", "arbitrary"), vmem),
        name="fox_attention",
    )(*args)


def _neg_expm1(y, exp_half_y):
    series = 1.0 + y * (1.0 / 7.0)
    for k in (6.0, 5.0, 4.0, 3.0, 2.0):
        series = 1.0 + (y * (1.0 / k)) * series
    return jnp.where(y > -0.25, -(y * series), 1.0 - exp_half_y * exp_half_y)


def _rglru_kernel(*refs, tt, has_state):
    if has_state:
        (xr_ref, yr_ref, ch_ref, h0_ref, cw_ref, cb_ref, wg_ref, ba_ref, bx_ref, lam_ref,
         o_ref, hl_ref, ext_sc, a_sc, b_sc, h_sc, hc_sc) = refs
    else:
        (xr_ref, yr_ref, cw_ref, cb_ref, wg_ref, ba_ref, bx_ref, lam_ref,
         o_ref, hl_ref, ext_sc, a_sc, b_sc, h_sc, hc_sc) = refs
    sub = V7X_SUBLANES
    d = xr_ref.shape[1]

    @pl.when(pl.program_id(1) == 0)
    def _init_state():
        if has_state:
            ext_sc[0:sub, :] = ch_ref[...]
            hc_sc[...] = h0_ref[...]
        else:
            ext_sc[0:sub, :] = jnp.zeros((sub, d), F32)
            hc_sc[...] = jnp.zeros((1, d), F32)

    ext_sc[sub:sub + tt, :] = xr_ref[...].astype(F32)
    cw = cw_ref[...]
    xc = ext_sc[sub - 3:sub - 3 + tt, :] * cw[0:1, :]
    for w in range(1, CONV_W):
        xc = xc + ext_sc[sub - 3 + w:sub - 3 + w + tt, :] * cw[w:w + 1, :]
    xc = cb_ref[...] + xc
    ext_sc[0:sub, :] = ext_sc[tt:tt + sub, :]

    xcb = xc.astype(BF16)
    wb = V7X_MXU_DIM
    row_in_group = lax.broadcasted_iota(jnp.int32, (tt, wb), 0) & (sub - 1)
    for j in range(d // wb):
        cols = slice(j * wb, (j + 1) * wb)
        xcj = xc[:, cols]
        g = jnp.dot(xcb[:, cols], wg_ref[j], preferred_element_type=F32)
        r = jax.nn.sigmoid(g[:, :wb] + ba_ref[:, cols])
        i = jax.nn.sigmoid(g[:, wb:] + bx_ref[:, cols])
        log_a = (-RG_C * r) * _softplus(-lam_ref[:, cols])
        a = jnp.exp(log_a)
        bx = jnp.sqrt(_neg_expm1(2.0 * log_a, a)) * (i * xcj)
        for shift in (1, 2, 4):
            keep = row_in_group >= shift
            a_prev = pltpu.roll(a, shift, 0)
            b_prev = pltpu.roll(bx, shift, 0)
            bx = jnp.where(keep, a * b_prev + bx, bx)
            a = jnp.where(keep, a * a_prev, a)
        a_sc[:, cols] = a
        b_sc[:, cols] = bx

    hc = hc_sc[...]
    for grp in range(tt // sub):
        rows = slice(grp * sub, (grp + 1) * sub)
        h = a_sc[rows, :] * hc + b_sc[rows, :]
        h_sc[rows, :] = h
        hc = h[sub - 1:sub, :]
    hc_sc[...] = hc
    hl_ref[...] = hc
    o_ref[...] = (h_sc[...] * _gelu_tanh(yr_ref[...].astype(F32))).astype(o_ref.dtype)


def _rglru(xr, yr, conv_hist, h0, cw, cb, wg, ba, bx, lam):
    b, t, d = xr.shape
    tt = min(RNN_TIME_TILE, t)
    assert t % tt == 0 and tt % V7X_SUBLANES == 0
    has_state = conv_hist is not None
    tok_spec = pl.BlockSpec((None, tt, d), lambda i, j: (i, j, 0))
    in_specs = [tok_spec, tok_spec]
    args = [xr, yr]
    if has_state:
        in_specs += [pl.BlockSpec((None, V7X_SUBLANES, d), lambda i, j: (i, 0, 0)),
                     pl.BlockSpec((None, 1, d), lambda i, j: (i, 0, 0))]
        args += [conv_hist, h0]
    weights = [cw, cb, wg, ba, bx, lam]
    in_specs += [_const_spec(w.shape) for w in weights]
    args += weights
    vmem = 4 * tt * d * 2 * 2 + 2 * tt * d * 2 + 4 * (tt + 8) * d * 4 + 16 * tt * d * 4
    return pl.pallas_call(
        functools.partial(_rglru_kernel, tt=tt, has_state=has_state),
        grid=(b, t // tt),
        in_specs=in_specs,
        out_specs=(tok_spec, pl.BlockSpec((None, 1, d), lambda i, j: (i, 0, 0))),
        out_shape=(jax.ShapeDtypeStruct((b, t, d), BF16), jax.ShapeDtypeStruct((b, 1, d), F32)),
        scratch_shapes=[pltpu.VMEM((tt + V7X_SUBLANES, d), F32), pltpu.VMEM((tt, d), F32),
                        pltpu.VMEM((tt, d), F32), pltpu.VMEM((tt, d), F32), pltpu.VMEM((1, d), F32)],
        compiler_params=_params(("arbitrary", "arbitrary"), vmem),
        name="rglru",
    )(*args)


def _out_mlp_kernel(x_ref, oa_ref, or_ref, ga_ref, gb_ref, wpa_ref, wpr_ref, wo_ref, g2_ref,
                    wup_ref, wdn_ref, gf_ref, y_ref):
    d = x_ref.shape[1]
    ya = jnp.dot(oa_ref[...], wpa_ref[...], preferred_element_type=F32)
    yb = jnp.dot(or_ref[...], wpr_ref[...], preferred_element_type=F32)
    merged = (jax.nn.sigmoid(ga_ref[...].astype(F32)) * ya
              + jax.nn.sigmoid(gb_ref[...].astype(F32)) * yb)
    x1 = x_ref[...] + jnp.dot(merged.astype(BF16), wo_ref[...], preferred_element_type=F32)
    h2 = _rmsnorm(x1, g2_ref[...]).astype(BF16)
    acc = x1
    for j in range(wup_ref.shape[1] // d):
        cols = slice(j * d, (j + 1) * d)
        u = jnp.maximum(jnp.dot(h2, wup_ref[:, cols], preferred_element_type=F32), 0.0)
        acc = acc + jnp.dot((u * u).astype(BF16), wdn_ref[cols, :], preferred_element_type=F32)
    y_ref[...] = _rmsnorm(acc, gf_ref[...])


def _out_mlp(x2d, oa, orn, ga, gb, wpa, wpr, wo, g2, wup, wdn, gf):
    n, d = x2d.shape
    tm = min(TOKEN_TILE, n)
    assert n % tm == 0
    tok_spec = pl.BlockSpec((tm, d), lambda i: (i, 0))
    weights = [wpa, wpr, wo, g2, wup, wdn, gf]
    vmem = (sum(w.size * w.dtype.itemsize for w in weights) + 2 * tm * d * (4 + 4 * 2 + 4)
            + 10 * tm * d * 4)
    return pl.pallas_call(
        _out_mlp_kernel,
        grid=(n // tm,),
        in_specs=[tok_spec] * 5 + [_const_spec(w.shape) for w in weights],
        out_specs=tok_spec,
        out_shape=jax.ShapeDtypeStruct((n, d), F32),
        compiler_params=_params(("arbitrary",), vmem),
        name="out_mlp",
    )(x2d, oa, orn, ga, gb, *weights)


def _prepare_weights(norm_mix_g, w_in, b_f, conv_w, conv_b, w_rg_a, b_rg_a, w_rg_x, b_rg_x, rg_lambda,
                     w_proj_attn, w_proj_rnn, w_out, norm_mlp_g, w_up, w_down, norm_final_g):
    d = w_in.shape[0]
    d_attn = N_HEADS * HEAD_DIM
    qkv_end = 3 * d_attn
    rest = qkv_end + N_HEADS
    groups = [w_in[:, :d_attn] * (HEAD_DIM ** -0.5),
              w_in[:, d_attn:2 * d_attn], w_in[:, 2 * d_attn:qkv_end]]
    groups += [w_in[:, rest + s * d:rest + (s + 1) * d] for s in range(4)]
    w_cat = jnp.stack(groups).astype(BF16)
    pad = V7X_LANES - N_HEADS
    wf = jnp.pad(w_in[:, qkv_end:rest], ((0, 0), (0, pad))).astype(BF16)
    bf = jnp.pad(b_f, (0, pad)).reshape(1, V7X_LANES)

    def block_diag(w):
        n_tiles = w.shape[0] // GATE_BLOCKS_PER_TILE
        w4 = w.reshape(n_tiles, GATE_BLOCKS_PER_TILE, RNN_BLOCK, RNN_BLOCK)
        eye = jnp.eye(GATE_BLOCKS_PER_TILE, dtype=w.dtype)
        return jnp.einsum('jmde,mn->jmdne', w4, eye).reshape(n_tiles, V7X_MXU_DIM, V7X_MXU_DIM)

    wg = jnp.concatenate([block_diag(w_rg_a), block_diag(w_rg_x)], axis=-1).astype(BF16)
    row = lambda v: v.reshape(1, -1)
    return dict(
        in_proj=(row(norm_mix_g), w_cat, wf, bf),
        rglru=(conv_w, row(conv_b), wg, row(b_rg_a), row(b_rg_x), row(rg_lambda)),
        out_mlp=(w_proj_attn.astype(BF16), w_proj_rnn.astype(BF16), w_out.astype(BF16), row(norm_mlp_g),
                 w_up.astype(BF16), w_down.astype(BF16), row(norm_final_g)))


def _layer(x, k_hist, v_hist, logf_hist, conv_hist, h0, wts):
    b, t, d = x.shape
    x2d = x.reshape(b * t, d)
    q, k, v, xr, yr, ga, gb, lf, lft, tail = _in_proj(x2d, *wts['in_proj'], seq_len=t)
    as_seq = lambda a: a.reshape(b, t, d)
    lf_new = lf.reshape(b, t, N_HEADS)
    lft_new = lft.reshape(N_HEADS, b, t).transpose(1, 0, 2)
    has_hist = k_hist is not None
    if has_hist:
        past = k_hist.shape[1]
        lf_all = jnp.concatenate([logf_hist, lf_new], axis=1)
        lft_all = jnp.concatenate([logf_hist.transpose(0, 2, 1), lft_new], axis=2)
        k_hist = k_hist.reshape(b, past, d)
        v_hist = v_hist.reshape(b, past, d)
        conv_hist = jnp.pad(conv_hist, ((0, 0), (V7X_SUBLANES - (CONV_W - 1), 0), (0, 0)))
        h0 = h0.reshape(b, 1, d)
    else:
        lf_all, lft_all = lf_new, lft_new
    c, ct = _cumsum(lf_all, lft_all)
    o_attn = _attention(as_seq(q), as_seq(k), as_seq(v), k_hist, v_hist, c, ct)
    o_rnn, h_last = _rglru(as_seq(xr), as_seq(yr), conv_hist, h0, *wts['rglru'])
    y = _out_mlp(x2d, o_attn.reshape(b * t, d), o_rnn.reshape(b * t, d), ga, gb, *wts['out_mlp'])
    new_k = k.reshape(b, t, N_HEADS, HEAD_DIM)
    new_v = v.reshape(b, t, N_HEADS, HEAD_DIM)
    new_conv = tail[:, V7X_SUBLANES - (CONV_W - 1):, :]
    return y.reshape(b, t, d), new_k, new_v, lf_new, new_conv, h_last.reshape(b, d)


def kernel(x_prompt, x_sample, cache_k, cache_v, cache_logf, state_conv, state_rglru, norm_mix_g, w_in, b_f, conv_w, conv_b, w_rg_a, b_rg_a, w_rg_x, b_rg_x, rg_lambda, w_proj_attn, w_proj_rnn, w_out, norm_mlp_g, w_up, w_down, norm_final_g):
    assert w_in.shape[0] == 1, "the final norm is fused into the layer: one layer only"
    wts = _prepare_weights(norm_mix_g[0], w_in[0], b_f[0], conv_w[0], conv_b[0], w_rg_a[0], b_rg_a[0],
                           w_rg_x[0], b_rg_x[0], rg_lambda[0], w_proj_attn[0], w_proj_rnn[0], w_out[0],
                           norm_mlp_g[0], w_up[0], w_down[0], norm_final_g)
    yp, kp, vp, lp, cp, hp = _layer(x_prompt, None, None, None, None, None, wts)
    ys, ks, vs, ls, cs, hs = _layer(x_sample, cache_k[0], cache_v[0], cache_logf[0], state_conv[0],
                                    state_rglru[0], wts)
    lead = lambda a: a[None]
    return (yp, ys, lead(kp), lead(vp), lead(lp), lead(cp), lead(hp),
            lead(ks), lead(vs), lead(ls), lead(cs), lead(hs))
```

```python
import functools

import jax
import jax.numpy as jnp
from jax import lax
from jax.experimental import pallas as pl
from jax.experimental.pallas import tpu as pltpu

F32 = jnp.float32
BF16 = jnp.bfloat16

N_HEADS = 16
HEAD_DIM = 64
CONV_W = 4
RNN_BLOCK = 64
RG_C = 8.0
EPS = 1e-6
LOG2_E = 1.4426950408889634

V7X_LANES = 128
V7X_SUBLANES = 8
V7X_MXU_DIM = 256
V7X_VMEM_BYTES = 64 * 1024 * 1024

HEADS_PER_STEP = V7X_LANES // HEAD_DIM
GATE_BLOCKS_PER_TILE = V7X_MXU_DIM // RNN_BLOCK
N_PROJ = 7
TOKEN_TILE = 512
ATTN_Q_TILE = 512
ATTN_K_TILE = 512
RNN_TIME_TILE = 256
CUMSUM_BLOCK = 256


def _vmem_limit(estimate_bytes):
    return int(min(estimate_bytes + (12 << 20), V7X_VMEM_BYTES - (6 << 20)))


def _params(semantics, vmem_estimate):
    return pltpu.CompilerParams(dimension_semantics=semantics,
                                vmem_limit_bytes=_vmem_limit(vmem_estimate))


def _const_spec(shape):
    zeros = (0,) * len(shape)
    return pl.BlockSpec(shape, lambda *_: zeros, pipeline_mode=pl.Buffered(1))


def _rmsnorm(x, g):
    x = x * lax.rsqrt(jnp.mean(x * x, axis=-1, keepdims=True) + EPS)
    return x * g


def _log_sigmoid(x):
    return jnp.minimum(x, 0.0) - jnp.log1p(jnp.exp(-jnp.abs(x)))


def _softplus(x):
    return jnp.maximum(x, 0.0) + jnp.log1p(jnp.exp(-jnp.abs(x)))


def _gelu_tanh(x):
    c = 0.7978845608028654
    return x * (0.5 * (1.0 + jnp.tanh(c * (x + 0.044715 * (x * x * x)))))


def _split3(x):
    hi = x.astype(BF16)
    r1 = x - hi.astype(F32)
    mid = r1.astype(BF16)
    lo = (r1 - mid.astype(F32)).astype(BF16)
    return hi, mid, lo


def _in_proj_kernel(x_ref, g_ref, w_ref, wf_ref, bf_ref,
                    q_ref, k_ref, v_ref, xr_ref, yr_ref, ga_ref, gb_ref, lf_ref, lft_ref, tail_ref,
                    *, seg_len):
    hn = _rmsnorm(x_ref[...], g_ref[...]).astype(BF16)
    tm = hn.shape[0]
    outs = (q_ref, k_ref, v_ref, xr_ref, yr_ref, ga_ref, gb_ref)
    for s, o_ref in enumerate(outs):
        r = jnp.dot(hn, w_ref[s], preferred_element_type=F32)
        o_ref[...] = r.astype(o_ref.dtype)
        if o_ref is xr_ref:
            for j in range(tm // seg_len):
                end = (j + 1) * seg_len
                tail_ref[j] = r[end - V7X_SUBLANES:end, :]
    zf = jnp.dot(hn, wf_ref[...], preferred_element_type=F32)
    lf = _log_sigmoid(zf + bf_ref[...])
    lf_ref[...] = lf[:, :N_HEADS]
    lft_ref[...] = lf.T[:N_HEADS, :]


def _in_proj(x2d, g, w_cat, wf, bf, seq_len):
    n, d = x2d.shape
    tm = min(TOKEN_TILE, n)
    assert n % tm == 0 and (seq_len % tm == 0 or tm % seq_len == 0)
    seg_len = min(seq_len, tm)
    nseg = tm // seg_len
    tiles_per_stream = max(1, seq_len // tm)
    n_streams = n // seq_len
    row = lambda i: (i, 0)
    tok_bf16 = jax.ShapeDtypeStruct((n, d), BF16)
    tok_f32 = jax.ShapeDtypeStruct((n, d), F32)
    out_shape = (tok_bf16, tok_f32, tok_f32, tok_bf16, tok_bf16, tok_bf16, tok_bf16,
                 jax.ShapeDtypeStruct((n, N_HEADS), F32),
                 jax.ShapeDtypeStruct((N_HEADS, n), F32),
                 jax.ShapeDtypeStruct((n_streams, V7X_SUBLANES, d), F32))
    tok_spec = pl.BlockSpec((tm, d), row)
    out_specs = (tok_spec,) * N_PROJ + (
        pl.BlockSpec((tm, N_HEADS), row),
        pl.BlockSpec((N_HEADS, tm), lambda i: (0, i)),
        pl.BlockSpec((nseg, V7X_SUBLANES, d), lambda i: (i // tiles_per_stream, 0, 0)))
    vmem = (w_cat.size + wf.size) * 2 + 2 * tm * d * 4 + 2 * tm * d * (2 * 4 + 5 * 2) + 6 * tm * d * 4
    return pl.pallas_call(
        functools.partial(_in_proj_kernel, seg_len=seg_len),
        grid=(n // tm,),
        in_specs=[tok_spec, _const_spec(g.shape), _const_spec(w_cat.shape),
                  _const_spec(wf.shape), _const_spec(bf.shape)],
        out_specs=out_specs,
        out_shape=out_shape,
        compiler_params=_params(("arbitrary",), vmem),
        name="in_proj",
    )(x2d, g, w_cat, wf, bf)


def _cumsum_kernel(lf_ref, lft_ref, c_ref, ct_ref):
    tk = lf_ref.shape[0]
    blk = min(CUMSUM_BLOCK, tk)
    row = lax.broadcasted_iota(jnp.int32, (blk, blk), 0)
    col = lax.broadcasted_iota(jnp.int32, (blk, blk), 1)
    lower = (row >= col).astype(BF16)
    upper = (row <= col).astype(BF16)
    carry = jnp.zeros((1, N_HEADS), F32)
    carry_t = jnp.zeros((N_HEADS, 1), F32)
    for s in range(0, tk, blk):
        sz = min(blk, tk - s)
        c = carry
        for part in _split3(lf_ref[s:s + sz, :]):
            c = c + jnp.dot(lower[:sz, :sz], part, preferred_element_type=F32)
        c_ref[s:s + sz, :] = c * LOG2_E
        carry = c[sz - 1:sz, :]
        ct = carry_t
        for part in _split3(lft_ref[:, s:s + sz]):
            ct = ct + jnp.dot(part, upper[:sz, :sz], preferred_element_type=F32)
        ct_ref[:, s:s + sz] = ct * LOG2_E
        carry_t = ct[:, sz - 1:sz]


def _cumsum(lf, lft):
    b, tk, h = lf.shape
    return pl.pallas_call(
        _cumsum_kernel,
        grid=(b,),
        in_specs=[pl.BlockSpec((None, tk, h), lambda i: (i, 0, 0)),
                  pl.BlockSpec((None, h, tk), lambda i: (i, 0, 0))],
        out_specs=(pl.BlockSpec((None, tk, h), lambda i: (i, 0, 0)),
                   pl.BlockSpec((None, h, tk), lambda i: (i, 0, 0))),
        out_shape=(jax.ShapeDtypeStruct((b, tk, h), F32), jax.ShapeDtypeStruct((b, h, tk), F32)),
        compiler_params=_params(("arbitrary",), 8 * tk * V7X_LANES * 4),
        name="logf_cumsum",
    )(lf, lft)


def _attn_kernel(*refs, tq, tkf, past, has_hist):
    if has_hist:
        q_ref, kn_ref, vn_ref, kh_ref, vh_ref, cq_ref, ckd_ref, ckf_ref, o_ref, k_sc, v_sc = refs
    else:
        q_ref, kn_ref, vn_ref, cq_ref, ckd_ref, ckf_ref, o_ref, k_sc, v_sc = refs
    qi = pl.program_id(2)
    t_new = kn_ref.shape[0]

    @pl.when(qi == 0)
    def _stage_keys():
        if has_hist:
            k_sc[0:past, :] = kh_ref[...].astype(BF16)
            v_sc[0:past, :] = vh_ref[...].astype(BF16)
        k_sc[past:past + t_new, :] = kn_ref[...].astype(BF16)
        v_sc[past:past + t_new, :] = vn_ref[...].astype(BF16)

    q = q_ref[...]
    lane = lax.broadcasted_iota(jnp.int32, q.shape, 1)
    first = lane < HEAD_DIM
    zero = jnp.zeros_like(q)
    q_heads = (jnp.where(first, q, zero), jnp.where(first, zero, q))
    cq = cq_ref[...]
    cq_heads = (cq[:, 0:1], cq[:, 1:2])

    def scores(qh, keys, cqh, ckh):
        s = lax.dot_general(qh, keys, (((1,), (1,)), ((), ())), preferred_element_type=F32)
        return (s + cqh) - ckh

    off = pl.multiple_of(past + qi * tq, tq)
    kd = k_sc[pl.ds(off, tq), :]
    vd = v_sc[pl.ds(off, tq), :]
    ckd = ckd_ref[...]
    causal = (lax.broadcasted_iota(jnp.int32, (tq, tq), 1)
              <= lax.broadcasted_iota(jnp.int32, (tq, tq), 0))
    carry = []
    for h in range(HEADS_PER_STEP):
        s = scores(q_heads[h], kd, cq_heads[h], ckd[h:h + 1, :])
        s = jnp.where(causal, s, -jnp.inf)
        m = jnp.max(s, axis=1, keepdims=True)
        p = jnp.exp2(s - m)
        l = jnp.sum(p, axis=1, keepdims=True)
        acc = jnp.dot(p.astype(BF16), vd, preferred_element_type=F32)
        carry += [m, l, acc]

    def full_chunk(j, carry):
        start = pl.multiple_of(j * tkf, tkf)
        ks = k_sc[pl.ds(start, tkf), :]
        vs = v_sc[pl.ds(start, tkf), :]
        ck = ckf_ref[j]
        out = []
        for h in range(HEADS_PER_STEP):
            m, l, acc = carry[3 * h:3 * h + 3]
            s = scores(q_heads[h], ks, cq_heads[h], ck[h:h + 1, :])
            m_new = jnp.maximum(m, jnp.max(s, axis=1, keepdims=True))
            alpha = jnp.exp2(m - m_new)
            p = jnp.exp2(s - m_new)
            l = alpha * l + jnp.sum(p, axis=1, keepdims=True)
            acc = alpha * acc + jnp.dot(p.astype(BF16), vs, preferred_element_type=F32)
            out += [m_new, l, acc]
        return tuple(out)

    n_full = (past + qi * tq) // tkf if not has_hist else past // tkf
    carry = lax.fori_loop(0, n_full, full_chunk, tuple(carry))
    o_a = carry[2] / carry[1]
    o_b = carry[5] / carry[4]
    o_ref[...] = jnp.where(first, o_a, o_b).astype(o_ref.dtype)


def _attention(q, k_new, v_new, k_hist, v_hist, c, ct):
    b, t, d = q.shape
    has_hist = k_hist is not None
    past = k_hist.shape[1] if has_hist else 0
    tk_all = past + t
    tq = min(ATTN_Q_TILE, t)
    tkf = ATTN_K_TILE
    nq = t // tq
    n_pairs = N_HEADS // HEADS_PER_STEP
    assert t % tq == 0 and ((has_hist and nq == 1 and past % tkf == 0) or (not has_hist and tq == tkf))
    nfc = past // tkf if has_hist else nq
    cq = c[:, past:, :].reshape(b, t, n_pairs, HEADS_PER_STEP).transpose(0, 2, 1, 3)
    ct4 = ct.reshape(b, n_pairs, HEADS_PER_STEP, tk_all)
    ckd = ct4[..., past:].reshape(b, n_pairs, HEADS_PER_STEP, nq, tq).transpose(0, 1, 3, 2, 4)
    ckf = ct4[..., :nfc * tkf].reshape(b, n_pairs, HEADS_PER_STEP, nfc, tkf).transpose(0, 1, 3, 2, 4)

    lanes = V7X_LANES
    new_spec = pl.BlockSpec((None, t, lanes), lambda i, p, j: (i, 0, p))
    in_specs = [pl.BlockSpec((None, tq, lanes), lambda i, p, j: (i, j, p)), new_spec, new_spec]
    args = [q, k_new, v_new]
    if has_hist:
        hist_spec = pl.BlockSpec((None, past, lanes), lambda i, p, j: (i, 0, p))
        in_specs += [hist_spec, hist_spec]
        args += [k_hist, v_hist]
    in_specs += [
        pl.BlockSpec((None, None, tq, HEADS_PER_STEP), lambda i, p, j: (i, p, j, 0)),
        pl.BlockSpec((None, None, None, HEADS_PER_STEP, tq), lambda i, p, j: (i, p, j, 0, 0)),
        pl.BlockSpec((None, None, nfc, HEADS_PER_STEP, tkf), lambda i, p, j: (i, p, 0, 0, 0)),
    ]
    args += [cq, ckd, ckf]
    vmem = (4 * tk_all * lanes * 4 + 2 * tk_all * lanes * 2 + 12 * tq * max(tq, tkf) * 4
            + 4 * tq * lanes * 4 + 2 * tq * lanes * 4)
    return pl.pallas_call(
        functools.partial(_attn_kernel, tq=tq, tkf=tkf, past=past, has_hist=has_hist),
        grid=(b, n_pairs, nq),
        in_specs=in_specs,
        out_specs=pl.BlockSpec((None, tq, lanes), lambda i, p, j: (i, j, p)),
        out_shape=jax.ShapeDtypeStruct((b, t, d), BF16),
        scratch_shapes=[pltpu.VMEM((tk_all, lanes), BF16), pltpu.VMEM((tk_all, lanes), BF16)],
        compiler_params=_params(("arbitrary", "arbitrary", "arbitrary"), vmem),
        name="fox_attention",
    )(*args)


N_SPLIT = 3


def _augment(own, lane, base, data, ones_first, parts):
    aug = jnp.zeros(data.shape, F32)
    for i, part in enumerate(parts):
        at = base + i + (N_SPLIT if ones_first else 0)
        aug = jnp.where(lane == at, part.astype(F32), aug)
    ones_at = base if ones_first else base + N_SPLIT
    aug = jnp.where((lane >= ones_at) & (lane < ones_at + N_SPLIT), 1.0, aug)
    return jnp.where(own, data.astype(F32), aug)


def _attn_first_kernel(q_ref, k_ref, v_ref, c_ref, o_ref, kt_sc, v_sc, *, tq):
    pair = pl.program_id(1)
    qi = pl.program_id(2)
    n_chunks = k_ref.shape[0] // tq
    lane = lax.broadcasted_iota(jnp.int32, (tq, V7X_LANES), 1)
    own = (lane < HEAD_DIM, lane >= HEAD_DIM)
    base = (HEAD_DIM, 0)

    def head_columns(c_rows):
        head = lax.broadcasted_iota(jnp.int32, c_rows.shape, 1)
        return [jnp.sum(jnp.where(head == HEADS_PER_STEP * pair + h, c_rows, 0.0), axis=1, keepdims=True)
                for h in range(HEADS_PER_STEP)]

    @pl.when(qi == 0)
    def _stage_keys():
        for j in range(n_chunks):
            rows = slice(j * tq, (j + 1) * tq)
            k = k_ref[rows, :]
            v = v_ref[rows, :]
            c_cols = head_columns(c_ref[rows, :])
            for h in range(HEADS_PER_STEP):
                neg_parts = [-part.astype(F32) for part in _split3(c_cols[h])]
                k_wide = _augment(own[h], lane, base[h], k, False, neg_parts)
                kt_sc[h, j] = k_wide.T.astype(BF16)
                v_sc[h, rows, :] = jnp.where(own[h], v, jnp.where(lane == base[h], 1.0, 0.0)).astype(BF16)

    q = q_ref[...]
    cq_cols = head_columns(c_ref[pl.ds(pl.multiple_of(qi * tq, tq), tq), :])
    q_wide = [_augment(own[h], lane, base[h], q, True, _split3(cq_cols[h])).astype(BF16)
              for h in range(HEADS_PER_STEP)]
    causal = (lax.broadcasted_iota(jnp.int32, (tq, tq), 1)
              <= lax.broadcasted_iota(jnp.int32, (tq, tq), 0))

    def chunk(j, carry, diagonal):
        out = []
        for h in range(HEADS_PER_STEP):
            s = jnp.dot(q_wide[h], kt_sc[h, j], preferred_element_type=F32)
            values = v_sc[h, pl.ds(pl.multiple_of(j * tq, tq), tq), :]
            if diagonal:
                s = jnp.where(causal, s, -jnp.inf)
                m_new = jnp.max(s, axis=1, keepdims=True)
                p = jnp.exp2(s - m_new).astype(BF16)
                acc = jnp.dot(p, values, preferred_element_type=F32)
            else:
                m, acc = carry[2 * h:2 * h + 2]
                m_new = jnp.maximum(m, jnp.max(s, axis=1, keepdims=True))
                p = jnp.exp2(s - m_new).astype(BF16)
                acc = jnp.exp2(m - m_new) * acc + jnp.dot(p, values, preferred_element_type=F32)
            out += [m_new, acc]
        return tuple(out)

    carry = chunk(qi, None, True)
    carry = lax.fori_loop(0, qi, lambda j, c: chunk(j, c, False), carry)
    out = [carry[2 * h + 1] / carry[2 * h + 1][:, base[h]:base[h] + 1] for h in range(HEADS_PER_STEP)]
    o_ref[...] = jnp.where(own[0], out[0], out[1]).astype(o_ref.dtype)


def _attention_first(q, k, v, c):
    b, t, d = q.shape
    tq = min(ATTN_Q_TILE, t)
    assert t % tq == 0
    n_pairs = N_HEADS // HEADS_PER_STEP
    lanes = V7X_LANES
    seq_spec = pl.BlockSpec((None, t, lanes), lambda i, p, j: (i, 0, p))
    q_spec = pl.BlockSpec((None, tq, lanes), lambda i, p, j: (i, j, p))
    vmem = (4 * t * lanes * 4 + 2 * t * lanes * 4 + 4 * t * lanes * 2 + 10 * tq * tq * 4
            + 8 * tq * lanes * 4)
    return pl.pallas_call(
        functools.partial(_attn_first_kernel, tq=tq),
        grid=(b, n_pairs, t // tq),
        in_specs=[q_spec, seq_spec, seq_spec, pl.BlockSpec((None, t, N_HEADS), lambda i, p, j: (i, 0, 0))],
        out_specs=q_spec,
        out_shape=jax.ShapeDtypeStruct((b, t, d), BF16),
        scratch_shapes=[pltpu.VMEM((HEADS_PER_STEP, t // tq, lanes, tq), BF16),
                        pltpu.VMEM((HEADS_PER_STEP, t, lanes), BF16)],
        compiler_params=_params(("arbitrary", "arbitrary", "arbitrary"), vmem),
        name="fox_attention_first",
    )(q, k, v, c)


def _neg_expm1(y, exp_half_y):
    series = y * (-1.0 / 120.0) - 1.0 / 24.0
    for coeff in (-1.0 / 6.0, -0.5, -1.0):
        series = series * y + coeff
    return jnp.where(y > -0.0625, y * series, 1.0 - exp_half_y * exp_half_y)


def _scan_groups(a, b):
    rows, width = a.shape
    sub = V7X_SUBLANES
    a = a.reshape(rows // sub, sub, width)
    b = b.reshape(rows // sub, sub, width)
    row = lax.broadcasted_iota(jnp.int32, a.shape, 1)
    for shift in (1, 2, 4):
        keep = row >= shift
        a_prev = jnp.where(keep, pltpu.roll(a, shift, 1), 1.0)
        b_prev = jnp.where(keep, pltpu.roll(b, shift, 1), 0.0)
        b = a * b_prev + b
        a = a * a_prev
    return a.reshape(rows, width), b.reshape(rows, width)


def _rglru_kernel(*refs, tt, has_state):
    if has_state:
        (xr_ref, yr_ref, ch_ref, h0_ref, cw_ref, cb_ref, wg_ref, ba_ref, bx_ref, lam_ref,
         o_ref, hl_ref, ext_sc, a_sc, b_sc, h_sc, hc_sc) = refs
    else:
        (xr_ref, yr_ref, cw_ref, cb_ref, wg_ref, ba_ref, bx_ref, lam_ref,
         o_ref, hl_ref, ext_sc, a_sc, b_sc, h_sc, hc_sc) = refs
    sub = V7X_SUBLANES
    d = xr_ref.shape[1]

    @pl.when(pl.program_id(1) == 0)
    def _init_state():
        if has_state:
            ext_sc[0:sub, :] = ch_ref[...]
            hc_sc[...] = h0_ref[...]
        else:
            ext_sc[0:sub, :] = jnp.zeros((sub, d), F32)
            hc_sc[...] = jnp.zeros((1, d), F32)

    ext_sc[sub:sub + tt, :] = xr_ref[...].astype(F32)
    cw = cw_ref[...]
    xc = ext_sc[sub - 3:sub - 3 + tt, :] * cw[0:1, :]
    for w in range(1, CONV_W):
        xc = xc + ext_sc[sub - 3 + w:sub - 3 + w + tt, :] * cw[w:w + 1, :]
    xc = cb_ref[...] + xc
    ext_sc[0:sub, :] = ext_sc[tt:tt + sub, :]

    xcb = xc.astype(BF16)
    wb = V7X_MXU_DIM
    for j in range(d // wb):
        cols = slice(j * wb, (j + 1) * wb)
        xcj = xc[:, cols]
        g = jnp.dot(xcb[:, cols], wg_ref[j], preferred_element_type=F32)
        r = jax.nn.sigmoid(g[:, :wb] + ba_ref[:, cols])
        i = jax.nn.sigmoid(g[:, wb:] + bx_ref[:, cols])
        log_a = (-RG_C * r) * _softplus(-lam_ref[:, cols])
        a = jnp.exp(log_a)
        bx = jnp.sqrt(_neg_expm1(2.0 * log_a, a)) * (i * xcj)
        a_sc[:, cols], b_sc[:, cols] = _scan_groups(a, bx)

    hc = hc_sc[...]
    for grp in range(tt // sub):
        rows = slice(grp * sub, (grp + 1) * sub)
        h = a_sc[rows, :] * hc + b_sc[rows, :]
        h_sc[rows, :] = h
        hc = h[sub - 1:sub, :]
    hc_sc[...] = hc
    hl_ref[...] = hc
    o_ref[...] = (h_sc[...] * _gelu_tanh(yr_ref[...].astype(F32))).astype(o_ref.dtype)


def _rglru(xr, yr, conv_hist, h0, cw, cb, wg, ba, bx, lam):
    b, t, d = xr.shape
    tt = min(RNN_TIME_TILE, t)
    assert t % tt == 0 and tt % V7X_SUBLANES == 0
    has_state = conv_hist is not None
    tok_spec = pl.BlockSpec((None, tt, d), lambda i, j: (i, j, 0))
    in_specs = [tok_spec, tok_spec]
    args = [xr, yr]
    if has_state:
        in_specs += [pl.BlockSpec((None, V7X_SUBLANES, d), lambda i, j: (i, 0, 0)),
                     pl.BlockSpec((None, 1, d), lambda i, j: (i, 0, 0))]
        args += [conv_hist, h0]
    weights = [cw, cb, wg, ba, bx, lam]
    in_specs += [_const_spec(w.shape) for w in weights]
    args += weights
    vmem = 4 * tt * d * 2 * 2 + 2 * tt * d * 2 + 4 * (tt + 8) * d * 4 + 16 * tt * d * 4
    return pl.pallas_call(
        functools.partial(_rglru_kernel, tt=tt, has_state=has_state),
        grid=(b, t // tt),
        in_specs=in_specs,
        out_specs=(tok_spec, pl.BlockSpec((None, 1, d), lambda i, j: (i, 0, 0))),
        out_shape=(jax.ShapeDtypeStruct((b, t, d), BF16), jax.ShapeDtypeStruct((b, 1, d), F32)),
        scratch_shapes=[pltpu.VMEM((tt + V7X_SUBLANES, d), F32), pltpu.VMEM((tt, d), F32),
                        pltpu.VMEM((tt, d), F32), pltpu.VMEM((tt, d), F32), pltpu.VMEM((1, d), F32)],
        compiler_params=_params(("arbitrary", "arbitrary"), vmem),
        name="rglru",
    )(*args)


def _out_mlp_kernel(x_ref, oa_ref, or_ref, ga_ref, gb_ref, wpa_ref, wpr_ref, wo_ref, g2_ref,
                    wup_ref, wdn_ref, gf_ref, y_ref):
    d = x_ref.shape[1]
    ya = jnp.dot(oa_ref[...], wpa_ref[...], preferred_element_type=F32)
    yb = jnp.dot(or_ref[...], wpr_ref[...], preferred_element_type=F32)
    merged = (jax.nn.sigmoid(ga_ref[...].astype(F32)) * ya
              + jax.nn.sigmoid(gb_ref[...].astype(F32)) * yb)
    x1 = x_ref[...] + jnp.dot(merged.astype(BF16), wo_ref[...], preferred_element_type=F32)
    h2 = _rmsnorm(x1, g2_ref[...]).astype(BF16)
    acc = x1
    for j in range(wup_ref.shape[1] // d):
        cols = slice(j * d, (j + 1) * d)
        u = jnp.maximum(jnp.dot(h2, wup_ref[:, cols], preferred_element_type=F32), 0.0)
        acc = acc + jnp.dot((u * u).astype(BF16), wdn_ref[cols, :], preferred_element_type=F32)
    y_ref[...] = _rmsnorm(acc, gf_ref[...])


def _out_mlp(x2d, oa, orn, ga, gb, wpa, wpr, wo, g2, wup, wdn, gf):
    n, d = x2d.shape
    tm = min(TOKEN_TILE, n)
    assert n % tm == 0
    tok_spec = pl.BlockSpec((tm, d), lambda i: (i, 0))
    weights = [wpa, wpr, wo, g2, wup, wdn, gf]
    vmem = (sum(w.size * w.dtype.itemsize for w in weights) + 2 * tm * d * (4 + 4 * 2 + 4)
            + 10 * tm * d * 4)
    return pl.pallas_call(
        _out_mlp_kernel,
        grid=(n // tm,),
        in_specs=[tok_spec] * 5 + [_const_spec(w.shape) for w in weights],
        out_specs=tok_spec,
        out_shape=jax.ShapeDtypeStruct((n, d), F32),
        compiler_params=_params(("arbitrary",), vmem),
        name="out_mlp",
    )(x2d, oa, orn, ga, gb, *weights)


def _prepare_weights(norm_mix_g, w_in, b_f, conv_w, conv_b, w_rg_a, b_rg_a, w_rg_x, b_rg_x, rg_lambda,
                     w_proj_attn, w_proj_rnn, w_out, norm_mlp_g, w_up, w_down, norm_final_g):
    d = w_in.shape[0]
    d_attn = N_HEADS * HEAD_DIM
    qkv_end = 3 * d_attn
    rest = qkv_end + N_HEADS
    groups = [w_in[:, :d_attn] * (HEAD_DIM ** -0.5 * LOG2_E),
              w_in[:, d_attn:2 * d_attn], w_in[:, 2 * d_attn:qkv_end]]
    groups += [w_in[:, rest + s * d:rest + (s + 1) * d] for s in range(4)]
    w_cat = jnp.stack(groups).astype(BF16)
    pad = V7X_LANES - N_HEADS
    wf = jnp.pad(w_in[:, qkv_end:rest], ((0, 0), (0, pad))).astype(BF16)
    bf = jnp.pad(b_f, (0, pad)).reshape(1, V7X_LANES)

    def block_diag(w):
        n_tiles = w.shape[0] // GATE_BLOCKS_PER_TILE
        w4 = w.reshape(n_tiles, GATE_BLOCKS_PER_TILE, RNN_BLOCK, RNN_BLOCK)
        eye = jnp.eye(GATE_BLOCKS_PER_TILE, dtype=w.dtype)
        return jnp.einsum('jmde,mn->jmdne', w4, eye).reshape(n_tiles, V7X_MXU_DIM, V7X_MXU_DIM)

    wg = jnp.concatenate([block_diag(w_rg_a), block_diag(w_rg_x)], axis=-1).astype(BF16)
    row = lambda v: v.reshape(1, -1)
    return dict(
        in_proj=(row(norm_mix_g), w_cat, wf, bf),
        rglru=(conv_w, row(conv_b), wg, row(b_rg_a), row(b_rg_x), row(rg_lambda)),
        out_mlp=(w_proj_attn.astype(BF16), w_proj_rnn.astype(BF16), w_out.astype(BF16), row(norm_mlp_g),
                 w_up.astype(BF16), w_down.astype(BF16), row(norm_final_g)))


def _layer(x, k_hist, v_hist, logf_hist, conv_hist, h0, wts):
    b, t, d = x.shape
    x2d = x.reshape(b * t, d)
    q, k, v, xr, yr, ga, gb, lf, lft, tail = _in_proj(x2d, *wts['in_proj'], seq_len=t)
    as_seq = lambda a: a.reshape(b, t, d)
    lf_new = lf.reshape(b, t, N_HEADS)
    lft_new = lft.reshape(N_HEADS, b, t).transpose(1, 0, 2)
    has_hist = k_hist is not None
    if has_hist:
        past = k_hist.shape[1]
        lf_all = jnp.concatenate([logf_hist, lf_new], axis=1)
        lft_all = jnp.concatenate([logf_hist.transpose(0, 2, 1), lft_new], axis=2)
        k_hist = k_hist.reshape(b, past, d)
        v_hist = v_hist.reshape(b, past, d)
        conv_hist = jnp.pad(conv_hist, ((0, 0), (V7X_SUBLANES - (CONV_W - 1), 0), (0, 0)))
        h0 = h0.reshape(b, 1, d)
    else:
        lf_all, lft_all = lf_new, lft_new
    c, ct = _cumsum(lf_all, lft_all)
    if has_hist:
        o_attn = _attention(as_seq(q), as_seq(k), as_seq(v), k_hist, v_hist, c, ct)
    else:
        o_attn = _attention_first(as_seq(q), as_seq(k), as_seq(v), c)
    o_rnn, h_last = _rglru(as_seq(xr), as_seq(yr), conv_hist, h0, *wts['rglru'])
    y = _out_mlp(x2d, o_attn.reshape(b * t, d), o_rnn.reshape(b * t, d), ga, gb, *wts['out_mlp'])
    new_k = k.reshape(b, t, N_HEADS, HEAD_DIM)
    new_v = v.reshape(b, t, N_HEADS, HEAD_DIM)
    new_conv = tail[:, V7X_SUBLANES - (CONV_W - 1):, :]
    return y.reshape(b, t, d), new_k, new_v, lf_new, new_conv, h_last.reshape(b, d)


def kernel(x_prompt, x_sample, cache_k, cache_v, cache_logf, state_conv, state_rglru, norm_mix_g, w_in, b_f, conv_w, conv_b, w_rg_a, b_rg_a, w_rg_x, b_rg_x, rg_lambda, w_proj_attn, w_proj_rnn, w_out, norm_mlp_g, w_up, w_down, norm_final_g):
    assert w_in.shape[0] == 1, "the final norm is fused into the layer: one layer only"
    wts = _prepare_weights(norm_mix_g[0], w_in[0], b_f[0], conv_w[0], conv_b[0], w_rg_a[0], b_rg_a[0],
                           w_rg_x[0], b_rg_x[0], rg_lambda[0], w_proj_attn[0], w_proj_rnn[0], w_out[0],
                           norm_mlp_g[0], w_up[0], w_down[0], norm_final_g)
    yp, kp, vp, lp, cp, hp = _layer(x_prompt, None, None, None, None, None, wts)
    ys, ks, vs, ls, cs, hs = _layer(x_sample, cache_k[0], cache_v[0], cache_logf[0], state_conv[0],
                                    state_rglru[0], wts)
    lead = lambda a: a[None]
    return (yp, ys, lead(kp), lead(vp), lead(lp), lead(cp), lead(hp),
            lead(ks), lead(vs), lead(ls), lead(cs), lead(hs))
```

```python
import functools

import numpy as np
import jax
import jax.numpy as jnp
from jax import lax
from jax.experimental import pallas as pl
from jax.experimental.pallas import tpu as pltpu

F32 = jnp.float32
BF16 = jnp.bfloat16

N_HEADS = 16
HEAD_DIM = 64
CONV_W = 4
RNN_BLOCK = 64
RG_C = 8.0
EPS = 1e-6
LOG2_E = 1.4426950408889634

V7X_LANES = 128
V7X_SUBLANES = 8
V7X_MXU_DIM = 256
V7X_VMEM_BYTES = 64 * 1024 * 1024

HEADS_PER_TILE = V7X_LANES // HEAD_DIM
N_PAIRS = N_HEADS // HEADS_PER_TILE
GATE_BLOCKS_PER_TILE = V7X_MXU_DIM // RNN_BLOCK
TOKEN_TILE = 512
ATTN_Q_TILE = 512
RNN_TIME_TILE = 256
CUMSUM_BLOCK = 256
N_SPLIT = 3


def _head_of(pair, slot):
    return pair + N_PAIRS * slot


def _vmem_limit(estimate_bytes):
    return int(min(estimate_bytes + (12 << 20), V7X_VMEM_BYTES - (6 << 20)))


def _params(semantics, vmem_estimate):
    return pltpu.CompilerParams(dimension_semantics=semantics,
                                vmem_limit_bytes=_vmem_limit(vmem_estimate))


def _const_spec(shape):
    zeros = (0,) * len(shape)
    return pl.BlockSpec(shape, lambda *_: zeros, pipeline_mode=pl.Buffered(1))


def _rmsnorm(x, g):
    x = x * lax.rsqrt(jnp.mean(x * x, axis=-1, keepdims=True) + EPS)
    return x * g


def _log_sigmoid(x):
    return jnp.minimum(x, 0.0) - jnp.log1p(jnp.exp(-jnp.abs(x)))


def _softplus(x):
    return jnp.maximum(x, 0.0) + jnp.log1p(jnp.exp(-jnp.abs(x)))


def _gelu_tanh(x):
    c = 0.7978845608028654
    return x * (0.5 * (1.0 + jnp.tanh(c * (x + 0.044715 * (x * x * x)))))


def _split3(x):
    hi = x.astype(BF16).astype(F32)
    r1 = x - hi
    mid = r1.astype(BF16).astype(F32)
    lo = (r1 - mid).astype(BF16).astype(F32)
    return hi, mid, lo


def _store_head_major(r, o_ref, slab_sc):
    tm = r.shape[0]
    sub = V7X_SUBLANES
    n_tiles = r.shape[1] // V7X_LANES
    for j in range(n_tiles):
        slab_sc[:, j * sub:(j + 1) * sub, :] = (
            r[:, j * V7X_LANES:(j + 1) * V7X_LANES].reshape(tm // sub, sub, V7X_LANES))
    for t in range(sub):
        rows = slab_sc[:, pl.ds(t, n_tiles, stride=sub), :]
        o_ref[:, t, 0:N_PAIRS, :] = rows[:, :, 0:HEAD_DIM]
        o_ref[:, t, N_PAIRS:N_HEADS, :] = rows[:, :, HEAD_DIM:V7X_LANES]


def _qkv_proj_kernel(x_ref, g_ref, w_ref, q_ref, kb_ref, vb_ref, k4_ref, v4_ref, slab_sc):
    hn = _rmsnorm(x_ref[...], g_ref[...]).astype(BF16)
    q_ref[...] = jnp.dot(hn, w_ref[0], preferred_element_type=F32).astype(q_ref.dtype)
    for s, (b_ref, o4_ref) in enumerate(((kb_ref, k4_ref), (vb_ref, v4_ref)), start=1):
        r = jnp.dot(hn, w_ref[s], preferred_element_type=F32)
        b_ref[...] = r.astype(b_ref.dtype)
        _store_head_major(r, o4_ref, slab_sc)


def _qkv_proj(x2d, g, w_qkv):
    n, d = x2d.shape
    tm = min(TOKEN_TILE, n)
    sub = V7X_SUBLANES
    assert n % tm == 0 and tm % sub == 0 and d == N_HEADS * HEAD_DIM
    tok_spec = pl.BlockSpec((tm, d), lambda i: (i, 0))
    state_spec = pl.BlockSpec((tm // sub, sub, N_HEADS, HEAD_DIM), lambda i: (i, 0, 0, 0))
    tok_bf16 = jax.ShapeDtypeStruct((n, d), BF16)
    state_f32 = jax.ShapeDtypeStruct((n // sub, sub, N_HEADS, HEAD_DIM), F32)
    vmem = (w_qkv.size * 2 + 2 * tm * d * 4 + 2 * 3 * tm * d * 2
            + 2 * 2 * tm * N_HEADS * V7X_LANES * 4 + 5 * tm * d * 4)
    return pl.pallas_call(
        _qkv_proj_kernel,
        grid=(n // tm,),
        in_specs=[tok_spec, _const_spec(g.shape), _const_spec(w_qkv.shape)],
        out_specs=(tok_spec, tok_spec, tok_spec, state_spec, state_spec),
        out_shape=(tok_bf16, tok_bf16, tok_bf16, state_f32, state_f32),
        scratch_shapes=[pltpu.VMEM((tm // sub, d // V7X_LANES * sub, V7X_LANES), F32)],
        compiler_params=_params(("arbitrary",), vmem),
        name="qkv_proj",
    )(x2d, g, w_qkv)


def _gate_proj_kernel(x_ref, g_ref, w_ref, wf_ref, bf_ref,
                      xr_ref, yr_ref, ga_ref, gb_ref, lf_ref, tail_ref, *, seg_len):
    hn = _rmsnorm(x_ref[...], g_ref[...]).astype(BF16)
    tm = hn.shape[0]
    for s, o_ref in enumerate((xr_ref, yr_ref, ga_ref, gb_ref)):
        r = jnp.dot(hn, w_ref[s], preferred_element_type=F32)
        o_ref[...] = r.astype(o_ref.dtype)
        if o_ref is xr_ref:
            for j in range(tm // seg_len):
                end = (j + 1) * seg_len
                tail_ref[j] = r[end - V7X_SUBLANES:end, :]
    zf = jnp.dot(hn, wf_ref[...], preferred_element_type=F32)
    lf_ref[...] = _log_sigmoid(zf + bf_ref[...])[:, :N_HEADS]


def _gate_proj(x2d, g, w_gate, wf, bf, seq_len):
    n, d = x2d.shape
    tm = min(TOKEN_TILE, n)
    assert n % tm == 0 and (seq_len % tm == 0 or tm % seq_len == 0)
    seg_len = min(seq_len, tm)
    nseg = tm // seg_len
    tiles_per_stream = max(1, seq_len // tm)
    row = lambda i: (i, 0)
    tok_bf16 = jax.ShapeDtypeStruct((n, d), BF16)
    tok_spec = pl.BlockSpec((tm, d), row)
    out_shape = (tok_bf16,) * 4 + (jax.ShapeDtypeStruct((n, N_HEADS), F32),
                                   jax.ShapeDtypeStruct((n // seq_len, V7X_SUBLANES, d), F32))
    out_specs = (tok_spec,) * 4 + (
        pl.BlockSpec((tm, N_HEADS), row),
        pl.BlockSpec((nseg, V7X_SUBLANES, d), lambda i: (i // tiles_per_stream, 0, 0)))
    vmem = (w_gate.size + wf.size) * 2 + 2 * tm * d * 4 + 2 * 4 * tm * d * 2 + 5 * tm * d * 4
    return pl.pallas_call(
        functools.partial(_gate_proj_kernel, seg_len=seg_len),
        grid=(n // tm,),
        in_specs=[tok_spec, _const_spec(g.shape), _const_spec(w_gate.shape),
                  _const_spec(wf.shape), _const_spec(bf.shape)],
        out_specs=out_specs,
        out_shape=out_shape,
        compiler_params=_params(("arbitrary",), vmem),
        name="gate_proj",
    )(x2d, g, w_gate, wf, bf)


def _cumsum_kernel(lf_ref, c_ref):
    tk = lf_ref.shape[0]
    blk = min(CUMSUM_BLOCK, tk)
    row = lax.broadcasted_iota(jnp.int32, (blk, blk), 0)
    col = lax.broadcasted_iota(jnp.int32, (blk, blk), 1)
    lower = (row >= col).astype(BF16)
    carry = jnp.zeros((1, N_HEADS), F32)
    for s in range(0, tk, blk):
        sz = min(blk, tk - s)
        c = carry
        for part in _split3(lf_ref[s:s + sz, :]):
            c = c + jnp.dot(lower[:sz, :sz], part.astype(BF16), preferred_element_type=F32)
        c_ref[s:s + sz, :] = c * LOG2_E
        carry = c[sz - 1:sz, :]


def _cumsum(lf):
    b, tk, h = lf.shape
    spec = pl.BlockSpec((None, tk, h), lambda i: (i, 0, 0))
    return pl.pallas_call(
        _cumsum_kernel,
        grid=(b,),
        in_specs=[spec],
        out_specs=spec,
        out_shape=jax.ShapeDtypeStruct((b, tk, h), F32),
        compiler_params=_params(("arbitrary",), 8 * tk * V7X_LANES * 4),
        name="logf_cumsum",
    )(lf)


def _augment(own, lane, base, data, ones_first, parts):
    aug = jnp.zeros(lane.shape, F32)
    for i, part in enumerate(parts):
        at = base + i + (N_SPLIT if ones_first else 0)
        aug = jnp.where(lane == at, part, aug)
    ones_at = base if ones_first else base + N_SPLIT
    aug = jnp.where((lane >= ones_at) & (lane < ones_at + N_SPLIT), 1.0, aug)
    return aug if data is None else jnp.where(own, data.astype(F32), aug)


def _attn_first_kernel(q_ref, k_ref, v_ref, c_ref, o_ref, kt_sc, v_sc, *, tq):
    pair = pl.program_id(1)
    qi = pl.program_id(2)
    n_chunks = k_ref.shape[0] // tq
    lane = lax.broadcasted_iota(jnp.int32, (tq, V7X_LANES), 1)
    own = (lane < HEAD_DIM, lane >= HEAD_DIM)
    base = (HEAD_DIM, 0)

    def head_columns(c_rows):
        head = lax.broadcasted_iota(jnp.int32, c_rows.shape, 1)
        return [jnp.sum(jnp.where(head == _head_of(pair, h), c_rows, 0.0), axis=1, keepdims=True)
                for h in range(HEADS_PER_TILE)]

    @pl.when(qi == 0)
    def _stage_keys():
        for j in range(n_chunks):
            rows = slice(j * tq, (j + 1) * tq)
            k = k_ref[rows, :]
            v = v_ref[rows, :]
            c_cols = head_columns(c_ref[rows, :])
            for h in range(HEADS_PER_TILE):
                neg_parts = [-part for part in _split3(c_cols[h])]
                k_wide = _augment(own[h], lane, base[h], k, False, neg_parts)
                kt_sc[h, j] = k_wide.T.astype(BF16)
                v_sc[h, rows, :] = jnp.where(own[h], v.astype(F32),
                                             jnp.where(lane == base[h], 1.0, 0.0)).astype(BF16)

    q = q_ref[...]
    cq_cols = head_columns(c_ref[pl.ds(pl.multiple_of(qi * tq, tq), tq), :])
    q_wide = [_augment(own[h], lane, base[h], q, True, _split3(cq_cols[h])).astype(BF16)
              for h in range(HEADS_PER_TILE)]
    causal = (lax.broadcasted_iota(jnp.int32, (tq, tq), 1)
              <= lax.broadcasted_iota(jnp.int32, (tq, tq), 0))

    def chunk(j, carry, diagonal):
        out = []
        for h in range(HEADS_PER_TILE):
            s = jnp.dot(q_wide[h], kt_sc[h, j], preferred_element_type=F32)
            values = v_sc[h, pl.ds(pl.multiple_of(j * tq, tq), tq), :]
            if diagonal:
                s = jnp.where(causal, s, -jnp.inf)
                m_new = jnp.max(s, axis=1, keepdims=True)
                p = jnp.exp2(s - m_new).astype(BF16)
                acc = jnp.dot(p, values, preferred_element_type=F32)
            else:
                m, acc = carry[2 * h:2 * h + 2]
                m_new = jnp.maximum(m, jnp.max(s, axis=1, keepdims=True))
                p = jnp.exp2(s - m_new).astype(BF16)
                acc = jnp.exp2(m - m_new) * acc + jnp.dot(p, values, preferred_element_type=F32)
            out += [m_new, acc]
        return tuple(out)

    carry = chunk(qi, None, True)
    carry = lax.fori_loop(0, qi, lambda j, c: chunk(j, c, False), carry)
    out = [carry[2 * h + 1] / carry[2 * h + 1][:, base[h]:base[h] + 1] for h in range(HEADS_PER_TILE)]
    o_ref[...] = jnp.where(own[0], out[0], out[1]).astype(o_ref.dtype)


def _attention_first(q, k, v, c):
    b, t, d = q.shape
    tq = min(ATTN_Q_TILE, t)
    assert t % tq == 0
    lanes = V7X_LANES
    seq_spec = pl.BlockSpec((None, t, lanes), lambda i, p, j: (i, 0, p))
    q_spec = pl.BlockSpec((None, tq, lanes), lambda i, p, j: (i, j, p))
    vmem = (4 * t * lanes * 2 + 2 * t * lanes * 4 + 4 * t * lanes * 2 + 10 * tq * tq * 4
            + 8 * tq * lanes * 4)
    return pl.pallas_call(
        functools.partial(_attn_first_kernel, tq=tq),
        grid=(b, N_PAIRS, t // tq),
        in_specs=[q_spec, seq_spec, seq_spec, pl.BlockSpec((None, t, N_HEADS), lambda i, p, j: (i, 0, 0))],
        out_specs=q_spec,
        out_shape=jax.ShapeDtypeStruct((b, t, d), BF16),
        scratch_shapes=[pltpu.VMEM((HEADS_PER_TILE, t // tq, lanes, tq), BF16),
                        pltpu.VMEM((HEADS_PER_TILE, t, lanes), BF16)],
        compiler_params=_params(("arbitrary", "arbitrary", "arbitrary"), vmem),
        name="fox_attention_first",
    )(q, k, v, c)


def _attn_hist_kernel(q_ref, kn_ref, vn_ref, kh_ref, vh_ref, c_ref, o_ref, k_sc, v_sc, *, past):
    t = q_ref.shape[0]
    tk = past + t
    lanes = V7X_LANES
    pad_rows = k_sc.shape[0] - tk
    lane = lax.broadcasted_iota(jnp.int32, (tk, lanes), 1)
    sq_row = lax.broadcasted_iota(jnp.int32, (lanes, lanes), 0)
    sq_col = lax.broadcasted_iota(jnp.int32, (lanes, lanes), 1)
    causal = (lax.broadcasted_iota(jnp.int32, (t, lanes), 1)
              <= lax.broadcasted_iota(jnp.int32, (t, lanes), 0))
    key_parts = _split3(c_ref[...])
    query_parts = _split3(c_ref[past:tk, :])
    lane_q = lax.broadcasted_iota(jnp.int32, (t, lanes), 1)
    ones_col =jnp.where(lane == HEAD_DIM, 1.0, 0.0).astype(BF16)
    zero_pad = jnp.zeros((pad_rows, lanes), BF16)
    k_sc[tk:tk + pad_rows, :] = zero_pad
    v_sc[tk:tk + pad_rows, :] = zero_pad

    for pair in range(N_PAIRS):
        cols = slice(pair * lanes, (pair + 1) * lanes)
        q_pair, kn_pair, vn_pair = q_ref[:, cols], kn_ref[:, cols], vn_ref[:, cols]
        o_pair = jnp.zeros((t, lanes), F32)
        for slot in range(HEADS_PER_TILE):
            head = _head_of(pair, slot)
            shift = slot * HEAD_DIM
            to_low = ((sq_row == sq_col + shift) & (sq_col < HEAD_DIM)).astype(BF16)
            to_slot = ((sq_col == sq_row + shift) & (sq_row < HEAD_DIM)).astype(BF16)
            low = lambda a: jnp.dot(a, to_low, preferred_element_type=F32)
            neg_parts = [-part[:, head:head + 1] for part in key_parts]
            k_sc[0:tk, :] = _augment(None, lane, HEAD_DIM, None, False, neg_parts).astype(BF16)
            v_sc[0:tk, :] = ones_col
            k_sc[0:past, 0:HEAD_DIM] = kh_ref[pl.ds(head, past, stride=N_HEADS), :].astype(BF16)
            v_sc[0:past, 0:HEAD_DIM] = vh_ref[pl.ds(head, past, stride=N_HEADS), :].astype(BF16)
            k_sc[past:tk, 0:HEAD_DIM] = low(kn_pair)[:, 0:HEAD_DIM].astype(BF16)
            v_sc[past:tk, 0:HEAD_DIM] = low(vn_pair)[:, 0:HEAD_DIM].astype(BF16)
            q_wide = _augment(lane_q < HEAD_DIM, lane_q, HEAD_DIM, low(q_pair), True,
                              [part[:, head:head + 1] for part in query_parts]).astype(BF16)
            nt = (((1,), (1,)), ((), ()))
            s_hist = lax.dot_general(q_wide, k_sc[0:past, :], nt, preferred_element_type=F32)
            s_new = lax.dot_general(q_wide, k_sc[past:past + lanes, :], nt, preferred_element_type=F32)
            s_new = jnp.where(causal, s_new, -jnp.inf)
            m = jnp.maximum(jnp.max(s_hist, axis=1, keepdims=True), jnp.max(s_new, axis=1, keepdims=True))
            acc = (jnp.dot(jnp.exp2(s_hist - m).astype(BF16), v_sc[0:past, :], preferred_element_type=F32)
                   + jnp.dot(jnp.exp2(s_new - m).astype(BF16), v_sc[past:past + lanes, :],
                             preferred_element_type=F32))
            o_head = (acc / acc[:, HEAD_DIM:HEAD_DIM + 1]).astype(BF16)
            o_pair = o_pair + jnp.dot(o_head, to_slot, preferred_element_type=F32)
        o_ref[:, cols] = o_pair.astype(o_ref.dtype)


def _attention_hist(q, k_new, v_new, k_hist, v_hist, c):
    b, t, d = q.shape
    past = k_hist.shape[1]
    lanes = V7X_LANES
    assert t <= lanes and t % V7X_SUBLANES == 0 and past % lanes == 0
    k_rows = k_hist.reshape(b, past * N_HEADS, HEAD_DIM)
    v_rows = v_hist.reshape(b, past * N_HEADS, HEAD_DIM)
    new_spec = pl.BlockSpec((None, t, d), lambda i: (i, 0, 0))
    hist_spec = pl.BlockSpec((None, past * N_HEADS, HEAD_DIM), lambda i: (i, 0, 0))
    vmem = (2 * 2 * past * N_HEADS * lanes * 4 + 8 * t * d * 2 + 2 * (past + t) * lanes * 4
            + 2 * (past + lanes) * lanes * 2 + 16 * (past + t) * lanes * 4)
    return pl.pallas_call(
        functools.partial(_attn_hist_kernel, past=past),
        grid=(b,),
        in_specs=[new_spec, new_spec, new_spec, hist_spec, hist_spec,
                  pl.BlockSpec((None, past + t, N_HEADS), lambda i: (i, 0, 0))],
        out_specs=new_spec,
        out_shape=jax.ShapeDtypeStruct((b, t, d), BF16),
        scratch_shapes=[pltpu.VMEM((past + lanes, lanes), BF16), pltpu.VMEM((past + lanes, lanes), BF16)],
        compiler_params=_params(("arbitrary",), vmem),
        name="fox_attention_hist",
    )(q, k_new, v_new, k_rows, v_rows, c)


def _neg_expm1(y, exp_half_y):
    series = y * (-1.0 / 120.0) - 1.0 / 24.0
    for coeff in (-1.0 / 6.0, -0.5, -1.0):
        series = series * y + coeff
    return jnp.where(y > -0.0625, y * series, 1.0 - exp_half_y * exp_half_y)


def _scan_groups(a, b):
    rows, width = a.shape
    sub = V7X_SUBLANES
    a = a.reshape(rows // sub, sub, width)
    b = b.reshape(rows // sub, sub, width)
    row = lax.broadcasted_iota(jnp.int32, a.shape, 1)
    for shift in (1, 2, 4):
        keep = row >= shift
        a_prev = jnp.where(keep, pltpu.roll(a, shift, 1), 1.0)
        b_prev = jnp.where(keep, pltpu.roll(b, shift, 1), 0.0)
        b = a * b_prev + b
        a = a * a_prev
    return a.reshape(rows, width), b.reshape(rows, width)


def _rglru_kernel(*refs, tt, has_state):
    if has_state:
        (xr_ref, yr_ref, ch_ref, h0_ref, cw_ref, cb_ref, wg_ref, ba_ref, bx_ref, lam_ref,
         o_ref, hl_ref, ext_sc, a_sc, b_sc, h_sc, hc_sc) = refs
    else:
        (xr_ref, yr_ref, cw_ref, cb_ref, wg_ref, ba_ref, bx_ref, lam_ref,
         o_ref, hl_ref, ext_sc, a_sc, b_sc, h_sc, hc_sc) = refs
    sub = V7X_SUBLANES
    d = xr_ref.shape[1]

    @pl.when(pl.program_id(1) == 0)
    def _init_state():
        if has_state:
            ext_sc[0:sub, :] = ch_ref[...]
            hc_sc[...] = h0_ref[...]
        else:
            ext_sc[0:sub, :] = jnp.zeros((sub, d), F32)
            hc_sc[...] = jnp.zeros((1, d), F32)

    ext_sc[sub:sub + tt, :] = xr_ref[...].astype(F32)
    cw = cw_ref[...]
    xc = ext_sc[sub - 3:sub - 3 + tt, :] * cw[0:1, :]
    for w in range(1, CONV_W):
        xc = xc + ext_sc[sub - 3 + w:sub - 3 + w + tt, :] * cw[w:w + 1, :]
    xc = cb_ref[...] + xc
    ext_sc[0:sub, :] = ext_sc[tt:tt + sub, :]

    xcb = xc.astype(BF16)
    wb = V7X_MXU_DIM
    for j in range(d // wb):
        cols = slice(j * wb, (j + 1) * wb)
        xcj = xc[:, cols]
        g = jnp.dot(xcb[:, cols], wg_ref[j], preferred_element_type=F32)
        r = jax.nn.sigmoid(g[:, :wb] + ba_ref[:, cols])
        i = jax.nn.sigmoid(g[:, wb:] + bx_ref[:, cols])
        log_a = (-RG_C * r) * _softplus(-lam_ref[:, cols])
        a = jnp.exp(log_a)
        bx = jnp.sqrt(_neg_expm1(2.0 * log_a, a)) * (i * xcj)
        a_sc[:, cols], b_sc[:, cols] = _scan_groups(a, bx)

    hc = hc_sc[...]
    for grp in range(tt // sub):
        rows = slice(grp * sub, (grp + 1) * sub)
        h = a_sc[rows, :] * hc + b_sc[rows, :]
        h_sc[rows, :] = h
        hc = h[sub - 1:sub, :]
    hc_sc[...] = hc
    hl_ref[...] = hc
    o_ref[...] = (h_sc[...] * _gelu_tanh(yr_ref[...].astype(F32))).astype(o_ref.dtype)


def _rglru(xr, yr, conv_hist, h0, cw, cb, wg, ba, bx, lam):
    b, t, d = xr.shape
    tt = min(RNN_TIME_TILE, t)
    assert t % tt == 0 and tt % V7X_SUBLANES == 0
    has_state = conv_hist is not None
    tok_spec = pl.BlockSpec((None, tt, d), lambda i, j: (i, j, 0))
    in_specs = [tok_spec, tok_spec]
    args = [xr, yr]
    if has_state:
        in_specs += [pl.BlockSpec((None, V7X_SUBLANES, d), lambda i, j: (i, 0, 0)),
                     pl.BlockSpec((None, 1, d), lambda i, j: (i, 0, 0))]
        args += [conv_hist, h0]
    weights = [cw, cb, wg, ba, bx, lam]
    in_specs += [_const_spec(w.shape) for w in weights]
    args += weights
    vmem = 4 * tt * d * 2 * 2 + 2 * tt * d * 2 + 4 * (tt + 8) * d * 4 + 16 * tt * d * 4
    return pl.pallas_call(
        functools.partial(_rglru_kernel, tt=tt, has_state=has_state),
        grid=(b, t // tt),
        in_specs=in_specs,
        out_specs=(tok_spec, pl.BlockSpec((None, 1, d), lambda i, j: (i, 0, 0))),
        out_shape=(jax.ShapeDtypeStruct((b, t, d), BF16), jax.ShapeDtypeStruct((b, 1, d), F32)),
        scratch_shapes=[pltpu.VMEM((tt + V7X_SUBLANES, d), F32), pltpu.VMEM((tt, d), F32),
                        pltpu.VMEM((tt, d), F32), pltpu.VMEM((tt, d), F32), pltpu.VMEM((1, d), F32)],
        compiler_params=_params(("arbitrary", "arbitrary"), vmem),
        name="rglru",
    )(*args)


def _out_mlp_kernel(x_ref, oa_ref, or_ref, ga_ref, gb_ref, wpa_ref, wpr_ref, wo_ref, g2_ref,
                    wup_ref, wdn_ref, gf_ref, y_ref):
    d = x_ref.shape[1]
    ya = jnp.dot(oa_ref[...], wpa_ref[...], preferred_element_type=F32)
    yb = jnp.dot(or_ref[...], wpr_ref[...], preferred_element_type=F32)
    merged = (jax.nn.sigmoid(ga_ref[...].astype(F32)) * ya
              + jax.nn.sigmoid(gb_ref[...].astype(F32)) * yb)
    x1 = x_ref[...] + jnp.dot(merged.astype(BF16), wo_ref[...], preferred_element_type=F32)
    h2 = _rmsnorm(x1, g2_ref[...]).astype(BF16)
    acc = x1
    for j in range(wup_ref.shape[1] // d):
        cols = slice(j * d, (j + 1) * d)
        u = jnp.maximum(jnp.dot(h2, wup_ref[:, cols], preferred_element_type=F32), 0.0)
        acc = acc + jnp.dot((u * u).astype(BF16), wdn_ref[cols, :], preferred_element_type=F32)
    y_ref[...] = _rmsnorm(acc, gf_ref[...])


def _out_mlp(x2d, oa, orn, ga, gb, wpa, wpr, wo, g2, wup, wdn, gf):
    n, d = x2d.shape
    tm = min(TOKEN_TILE, n)
    assert n % tm == 0
    tok_spec = pl.BlockSpec((tm, d), lambda i: (i, 0))
    weights = [wpa, wpr, wo, g2, wup, wdn, gf]
    vmem = (sum(w.size * w.dtype.itemsize for w in weights) + 2 * tm * d * (4 + 4 * 2 + 4)
            + 10 * tm * d * 4)
    return pl.pallas_call(
        _out_mlp_kernel,
        grid=(n // tm,),
        in_specs=[tok_spec] * 5 + [_const_spec(w.shape) for w in weights],
        out_specs=tok_spec,
        out_shape=jax.ShapeDtypeStruct((n, d), F32),
        compiler_params=_params(("arbitrary",), vmem),
        name="out_mlp",
    )(x2d, oa, orn, ga, gb, *weights)


def _prepare_weights(norm_mix_g, w_in, b_f, conv_w, conv_b, w_rg_a, b_rg_a, w_rg_x, b_rg_x, rg_lambda,
                     w_proj_attn, w_proj_rnn, w_out, norm_mlp_g, w_up, w_down, norm_final_g):
    d = w_in.shape[0]
    d_attn = N_HEADS * HEAD_DIM
    qkv_end = 3 * d_attn
    rest = qkv_end + N_HEADS
    order = np.array([_head_of(p, s) * HEAD_DIM + f for p in range(N_PAIRS)
                      for s in range(HEADS_PER_TILE) for f in range(HEAD_DIM)])
    w_q = w_in[:, :d_attn] * (HEAD_DIM ** -0.5 * LOG2_E)
    w_qkv = jnp.stack([w[:, order] for w in (w_q, w_in[:, d_attn:2 * d_attn],
                                             w_in[:, 2 * d_attn:qkv_end])]).astype(BF16)
    w_gate = jnp.stack([w_in[:, rest + s * d:rest + (s + 1) * d] for s in range(4)]).astype(BF16)
    pad = V7X_LANES - N_HEADS
    wf = jnp.pad(w_in[:, qkv_end:rest], ((0, 0), (0, pad))).astype(BF16)
    bf = jnp.pad(b_f, (0, pad)).reshape(1, V7X_LANES)

    def block_diag(w):
        n_tiles = w.shape[0] // GATE_BLOCKS_PER_TILE
        w4 = w.reshape(n_tiles, GATE_BLOCKS_PER_TILE, RNN_BLOCK, RNN_BLOCK)
        eye = jnp.eye(GATE_BLOCKS_PER_TILE, dtype=w.dtype)
        return jnp.einsum('jmde,mn->jmdne', w4, eye).reshape(n_tiles, V7X_MXU_DIM, V7X_MXU_DIM)

    wg = jnp.concatenate([block_diag(w_rg_a), block_diag(w_rg_x)], axis=-1).astype(BF16)
    row = lambda v: v.reshape(1, -1)
    return dict(
        qkv_proj=(row(norm_mix_g), w_qkv),
        gate_proj=(row(norm_mix_g), w_gate, wf, bf),
        rglru=(conv_w, row(conv_b), wg, row(b_rg_a), row(b_rg_x), row(rg_lambda)),
        out_mlp=(w_proj_attn[order, :].astype(BF16), w_proj_rnn.astype(BF16), w_out.astype(BF16),
                 row(norm_mlp_g), w_up.astype(BF16), w_down.astype(BF16), row(norm_final_g)))


def _layer(x, k_hist, v_hist, logf_hist, conv_hist, h0, wts):
    b, t, d = x.shape
    x2d = x.reshape(b * t, d)
    q, kb, vb, k4, v4 = _qkv_proj(x2d, *wts['qkv_proj'])
    xr, yr, ga, gb, lf, tail = _gate_proj(x2d, *wts['gate_proj'], seq_len=t)
    as_seq = lambda a: a.reshape(b, t, d)
    lf_new = lf.reshape(b, t, N_HEADS)
    if k_hist is None:
        o_attn = _attention_first(as_seq(q), as_seq(kb), as_seq(vb), _cumsum(lf_new))
    else:
        c = _cumsum(jnp.concatenate([logf_hist, lf_new], axis=1))
        o_attn = _attention_hist(as_seq(q), as_seq(kb), as_seq(vb), k_hist, v_hist, c)
        conv_hist = jnp.pad(conv_hist, ((0, 0), (V7X_SUBLANES - (CONV_W - 1), 0), (0, 0)))
        h0 = h0.reshape(b, 1, d)
    o_rnn, h_last = _rglru(as_seq(xr), as_seq(yr), conv_hist, h0, *wts['rglru'])
    y = _out_mlp(x2d, o_attn.reshape(b * t, d), o_rnn.reshape(b * t, d), ga, gb, *wts['out_mlp'])
    new_k = k4.reshape(b, t, N_HEADS, HEAD_DIM)
    new_v = v4.reshape(b, t, N_HEADS, HEAD_DIM)
    new_conv = tail[:, V7X_SUBLANES - (CONV_W - 1):, :]
    return y.reshape(b, t, d), new_k, new_v, lf_new, new_conv, h_last.reshape(b, d)


def kernel(x_prompt, x_sample, cache_k, cache_v, cache_logf, state_conv, state_rglru, norm_mix_g, w_in, b_f, conv_w, conv_b, w_rg_a, b_rg_a, w_rg_x, b_rg_x, rg_lambda, w_proj_attn, w_proj_rnn, w_out, norm_mlp_g, w_up, w_down, norm_final_g):
    assert w_in.shape[0] == 1, "the final norm is fused into the layer: one layer only"
    wts = _prepare_weights(norm_mix_g[0], w_in[0], b_f[0], conv_w[0], conv_b[0], w_rg_a[0], b_rg_a[0],
                           w_rg_x[0], b_rg_x[0], rg_lambda[0], w_proj_attn[0], w_proj_rnn[0], w_out[0],
                           norm_mlp_g[0], w_up[0], w_down[0], norm_final_g)
    yp, kp, vp, lp, cp, hp = _layer(x_prompt, None, None, None, None, None, wts)
    ys, ks, vs, ls, cs, hs = _layer(x_sample, cache_k[0], cache_v[0], cache_logf[0], state_conv[0],
                                    state_rglru[0], wts)
    lead = lambda a: a[None]
    return (yp, ys, lead(kp), lead(vp), lead(lp), lead(cp), lead(hp),
            lead(ks), lead(vs), lead(ls), lead(cs), lead(hs))
```

```python
import functools

import numpy as np
import jax
import jax.numpy as jnp
from jax import lax
from jax.experimental import pallas as pl
from jax.experimental.pallas import tpu as pltpu

F32 = jnp.float32
BF16 = jnp.bfloat16

N_HEADS = 16
HEAD_DIM = 64
CONV_W = 4
RNN_BLOCK = 64
RG_C = 8.0
EPS = 1e-6
LOG2_E = 1.4426950408889634

V7X_LANES = 128
V7X_SUBLANES = 8
V7X_MXU_DIM = 256
V7X_VMEM_BYTES = 64 * 1024 * 1024

HEADS_PER_TILE = V7X_LANES // HEAD_DIM
N_PAIRS = N_HEADS // HEADS_PER_TILE
GATE_BLOCKS_PER_TILE = V7X_MXU_DIM // RNN_BLOCK
TOKEN_TILE = 512
ATTN_Q_TILE = 512
RNN_TIME_TILE = 256
CUMSUM_BLOCK = 256
N_SPLIT = 3


def _head_of(pair, slot):
    return pair + N_PAIRS * slot


def _vmem_limit(estimate_bytes):
    return int(min(estimate_bytes + (12 << 20), V7X_VMEM_BYTES - (6 << 20)))


def _params(semantics, vmem_estimate):
    return pltpu.CompilerParams(dimension_semantics=semantics,
                                vmem_limit_bytes=_vmem_limit(vmem_estimate))


def _const_spec(shape):
    zeros = (0,) * len(shape)
    return pl.BlockSpec(shape, lambda *_: zeros, pipeline_mode=pl.Buffered(1))


def _rmsnorm(x, g):
    x = x * lax.rsqrt(jnp.mean(x * x, axis=-1, keepdims=True) + EPS)
    return x * g


def _log_sigmoid(x):
    return jnp.minimum(x, 0.0) - jnp.log1p(jnp.exp(-jnp.abs(x)))


def _softplus(x):
    return jnp.maximum(x, 0.0) + jnp.log1p(jnp.exp(-jnp.abs(x)))


def _gelu_tanh(x):
    c = 0.7978845608028654
    return x * (0.5 * (1.0 + jnp.tanh(c * (x + 0.044715 * (x * x * x)))))


def _split3(x):
    hi = x.astype(BF16).astype(F32)
    r1 = x - hi
    mid = r1.astype(BF16).astype(F32)
    lo = (r1 - mid).astype(BF16).astype(F32)
    return hi, mid, lo


def _store_head_major(r, o_ref, slab_sc):
    tm = r.shape[0]
    sub = V7X_SUBLANES
    n_tiles = r.shape[1] // V7X_LANES
    for j in range(n_tiles):
        slab_sc[:, j * sub:(j + 1) * sub, :] = (
            r[:, j * V7X_LANES:(j + 1) * V7X_LANES].reshape(tm // sub, sub, V7X_LANES))
    for t in range(sub):
        rows = slab_sc[:, pl.ds(t, n_tiles, stride=sub), :]
        o_ref[:, t, 0:N_PAIRS, :] = rows[:, :, 0:HEAD_DIM]
        o_ref[:, t, N_PAIRS:N_HEADS, :] = rows[:, :, HEAD_DIM:V7X_LANES]


def _qkv_proj_kernel(x_ref, g_ref, w_ref, q_ref, kb_ref, vb_ref, k4_ref, v4_ref, slab_sc):
    hn = _rmsnorm(x_ref[...], g_ref[...]).astype(BF16)
    q_ref[...] = jnp.dot(hn, w_ref[0], preferred_element_type=F32).astype(q_ref.dtype)
    for s, (b_ref, o4_ref) in enumerate(((kb_ref, k4_ref), (vb_ref, v4_ref)), start=1):
        r = jnp.dot(hn, w_ref[s], preferred_element_type=F32)
        b_ref[...] = r.astype(b_ref.dtype)
        _store_head_major(r, o4_ref, slab_sc)


def _qkv_proj(x2d, g, w_qkv):
    n, d = x2d.shape
    tm = min(TOKEN_TILE, n)
    sub = V7X_SUBLANES
    assert n % tm == 0 and tm % sub == 0 and d == N_HEADS * HEAD_DIM
    tok_spec = pl.BlockSpec((tm, d), lambda i: (i, 0))
    state_spec = pl.BlockSpec((tm // sub, sub, N_HEADS, HEAD_DIM), lambda i: (i, 0, 0, 0))
    tok_bf16 = jax.ShapeDtypeStruct((n, d), BF16)
    state_f32 = jax.ShapeDtypeStruct((n // sub, sub, N_HEADS, HEAD_DIM), F32)
    vmem = (w_qkv.size * 2 + 2 * tm * d * 4 + 2 * 3 * tm * d * 2
            + 2 * 2 * tm * N_HEADS * V7X_LANES * 4 + 5 * tm * d * 4)
    return pl.pallas_call(
        _qkv_proj_kernel,
        grid=(n // tm,),
        in_specs=[tok_spec, _const_spec(g.shape), _const_spec(w_qkv.shape)],
        out_specs=(tok_spec, tok_spec, tok_spec, state_spec, state_spec),
        out_shape=(tok_bf16, tok_bf16, tok_bf16, state_f32, state_f32),
        scratch_shapes=[pltpu.VMEM((tm // sub, d // V7X_LANES * sub, V7X_LANES), F32)],
        compiler_params=_params(("arbitrary",), vmem),
        name="qkv_proj",
    )(x2d, g, w_qkv)


def _gate_proj_kernel(x_ref, g_ref, w_ref, wf_ref, bf_ref,
                      xr_ref, yr_ref, ga_ref, gb_ref, lf_ref, tail_ref, *, seg_len):
    hn = _rmsnorm(x_ref[...], g_ref[...]).astype(BF16)
    tm = hn.shape[0]
    for s, o_ref in enumerate((xr_ref, yr_ref, ga_ref, gb_ref)):
        r = jnp.dot(hn, w_ref[s], preferred_element_type=F32)
        o_ref[...] = r.astype(o_ref.dtype)
        if o_ref is xr_ref:
            for j in range(tm // seg_len):
                end = (j + 1) * seg_len
                tail_ref[j] = r[end - V7X_SUBLANES:end, :]
    zf = jnp.dot(hn, wf_ref[...], preferred_element_type=F32)
    lf_ref[...] = _log_sigmoid(zf + bf_ref[...])[:, :N_HEADS]


def _gate_proj(x2d, g, w_gate, wf, bf, seq_len):
    n, d = x2d.shape
    tm = min(TOKEN_TILE, n)
    assert n % tm == 0 and (seq_len % tm == 0 or tm % seq_len == 0)
    seg_len = min(seq_len, tm)
    nseg = tm // seg_len
    tiles_per_stream = max(1, seq_len // tm)
    row = lambda i: (i, 0)
    tok_bf16 = jax.ShapeDtypeStruct((n, d), BF16)
    tok_spec = pl.BlockSpec((tm, d), row)
    out_shape = (tok_bf16,) * 4 + (jax.ShapeDtypeStruct((n, N_HEADS), F32),
                                   jax.ShapeDtypeStruct((n // seq_len, V7X_SUBLANES, d), F32))
    out_specs = (tok_spec,) * 4 + (
        pl.BlockSpec((tm, N_HEADS), row),
        pl.BlockSpec((nseg, V7X_SUBLANES, d), lambda i: (i // tiles_per_stream, 0, 0)))
    vmem = (w_gate.size + wf.size) * 2 + 2 * tm * d * 4 + 2 * 4 * tm * d * 2 + 5 * tm * d * 4
    return pl.pallas_call(
        functools.partial(_gate_proj_kernel, seg_len=seg_len),
        grid=(n // tm,),
        in_specs=[tok_spec, _const_spec(g.shape), _const_spec(w_gate.shape),
                  _const_spec(wf.shape), _const_spec(bf.shape)],
        out_specs=out_specs,
        out_shape=out_shape,
        compiler_params=_params(("arbitrary",), vmem),
        name="gate_proj",
    )(x2d, g, w_gate, wf, bf)


def _cumsum_kernel(lf_ref, c_ref):
    tk = lf_ref.shape[0]
    blk = min(CUMSUM_BLOCK, tk)
    row = lax.broadcasted_iota(jnp.int32, (blk, blk), 0)
    col = lax.broadcasted_iota(jnp.int32, (blk, blk), 1)
    lower = (row >= col).astype(BF16)
    carry = jnp.zeros((1, N_HEADS), F32)
    for s in range(0, tk, blk):
        sz = min(blk, tk - s)
        c = carry
        for part in _split3(lf_ref[s:s + sz, :]):
            c = c + jnp.dot(lower[:sz, :sz], part.astype(BF16), preferred_element_type=F32)
        c_ref[s:s + sz, :] = c * LOG2_E
        carry = c[sz - 1:sz, :]


def _cumsum(lf):
    b, tk, h = lf.shape
    spec = pl.BlockSpec((None, tk, h), lambda i: (i, 0, 0))
    return pl.pallas_call(
        _cumsum_kernel,
        grid=(b,),
        in_specs=[spec],
        out_specs=spec,
        out_shape=jax.ShapeDtypeStruct((b, tk, h), F32),
        compiler_params=_params(("arbitrary",), 8 * tk * V7X_LANES * 4),
        name="logf_cumsum",
    )(lf)


def _augment(own, lane, base, data, ones_first, parts):
    aug = jnp.zeros(lane.shape, F32)
    for i, part in enumerate(parts):
        at = base + i + (N_SPLIT if ones_first else 0)
        aug = jnp.where(lane == at, part, aug)
    ones_at = base if ones_first else base + N_SPLIT
    aug = jnp.where((lane >= ones_at) & (lane < ones_at + N_SPLIT), 1.0, aug)
    return aug if data is None else jnp.where(own, data.astype(F32), aug)


def _attn_first_kernel(q_ref, k_ref, v_ref, c_ref, o_ref, kt_sc, v_sc, *, tq):
    pair = pl.program_id(1)
    qi = pl.program_id(2)
    n_chunks = k_ref.shape[0] // tq
    lane = lax.broadcasted_iota(jnp.int32, (tq, V7X_LANES), 1)
    own = (lane < HEAD_DIM, lane >= HEAD_DIM)
    base = (HEAD_DIM, 0)

    def head_columns(c_rows):
        head = lax.broadcasted_iota(jnp.int32, c_rows.shape, 1)
        return [jnp.sum(jnp.where(head == _head_of(pair, h), c_rows, 0.0), axis=1, keepdims=True)
                for h in range(HEADS_PER_TILE)]

    def stage_keys():
        for j in range(n_chunks):
            rows = slice(j * tq, (j + 1) * tq)
            k = k_ref[rows, :]
            v = v_ref[rows, :]
            c_cols = head_columns(c_ref[rows, :])
            for h in range(HEADS_PER_TILE):
                neg_parts = [-part for part in _split3(c_cols[h])]
                k_wide = _augment(own[h], lane, base[h], k, False, neg_parts)
                kt_sc[h, j] = k_wide.T.astype(BF16)
                v_sc[h, rows, :] = jnp.where(own[h], v.astype(F32),
                                             jnp.where(lane == base[h], 1.0, 0.0)).astype(BF16)

    causal = (lax.broadcasted_iota(jnp.int32, (tq, tq), 1)
              <= lax.broadcasted_iota(jnp.int32, (tq, tq), 0))

    def query_tile(n):
        if n == 0:
            stage_keys()
        q = q_ref[...]
        cq_cols = head_columns(c_ref[n * tq:(n + 1) * tq, :])
        for h in range(HEADS_PER_TILE):
            q_wide = _augment(own[h], lane, base[h], q, True, _split3(cq_cols[h])).astype(BF16)
            s = jnp.dot(q_wide, kt_sc[h, n], preferred_element_type=F32)
            s = jnp.where(causal, s, -jnp.inf)
            m = jnp.max(s, axis=1, keepdims=True)
            p = jnp.exp2(s - m).astype(BF16)
            acc = jnp.dot(p, v_sc[h, n * tq:(n + 1) * tq, :], preferred_element_type=F32)
            for j in range(n):
                s = jnp.dot(q_wide, kt_sc[h, j], preferred_element_type=F32)
                m_new = jnp.maximum(m, jnp.max(s, axis=1, keepdims=True))
                p = jnp.exp2(s - m_new).astype(BF16)
                acc = (jnp.exp2(m - m_new) * acc
                       + jnp.dot(p, v_sc[h, j * tq:(j + 1) * tq, :], preferred_element_type=F32))
                m = m_new
            out = acc / acc[:, base[h]:base[h] + 1]
            lanes_h = slice(h * HEAD_DIM, (h + 1) * HEAD_DIM)
            o_ref[:, lanes_h] = out[:, lanes_h].astype(o_ref.dtype)

    for n in range(n_chunks):
        pl.when(qi == n)(functools.partial(query_tile, n))


def _attention_first(q, k, v, c):
    b, t, d = q.shape
    tq = min(ATTN_Q_TILE, t)
    assert t % tq == 0
    lanes = V7X_LANES
    seq_spec = pl.BlockSpec((None, t, lanes), lambda i, p, j: (i, 0, p))
    q_spec = pl.BlockSpec((None, tq, lanes), lambda i, p, j: (i, j, p))
    vmem = (4 * t * lanes * 2 + 2 * t * lanes * 4 + 4 * t * lanes * 2 + 10 * tq * tq * 4
            + 8 * tq * lanes * 4)
    return pl.pallas_call(
        functools.partial(_attn_first_kernel, tq=tq),
        grid=(b, N_PAIRS, t // tq),
        in_specs=[q_spec, seq_spec, seq_spec, pl.BlockSpec((None, t, N_HEADS), lambda i, p, j: (i, 0, 0))],
        out_specs=q_spec,
        out_shape=jax.ShapeDtypeStruct((b, t, d), BF16),
        scratch_shapes=[pltpu.VMEM((HEADS_PER_TILE, t // tq, lanes, tq), BF16),
                        pltpu.VMEM((HEADS_PER_TILE, t, lanes), BF16)],
        compiler_params=_params(("arbitrary", "arbitrary", "arbitrary"), vmem),
        name="fox_attention_first",
    )(q, k, v, c)


def _attn_hist_kernel(q_ref, kn_ref, vn_ref, kh_ref, vh_ref, c_ref, o_ref, k_sc, v_sc, *, past):
    t = q_ref.shape[0]
    tk = past + t
    lanes = V7X_LANES
    pad_rows = k_sc.shape[0] - tk
    lane = lax.broadcasted_iota(jnp.int32, (tk, lanes), 1)
    sq_row = lax.broadcasted_iota(jnp.int32, (lanes, lanes), 0)
    sq_col = lax.broadcasted_iota(jnp.int32, (lanes, lanes), 1)
    causal = (lax.broadcasted_iota(jnp.int32, (t, lanes), 1)
              <= lax.broadcasted_iota(jnp.int32, (t, lanes), 0))
    key_parts = _split3(c_ref[...])
    query_parts = _split3(c_ref[past:tk, :])
    lane_q = lax.broadcasted_iota(jnp.int32, (t, lanes), 1)
    ones_col =jnp.where(lane == HEAD_DIM, 1.0, 0.0).astype(BF16)
    zero_pad = jnp.zeros((pad_rows, lanes), BF16)
    k_sc[tk:tk + pad_rows, :] = zero_pad
    v_sc[tk:tk + pad_rows, :] = zero_pad

    for pair in range(N_PAIRS):
        cols = slice(pair * lanes, (pair + 1) * lanes)
        q_pair, kn_pair, vn_pair = q_ref[:, cols], kn_ref[:, cols], vn_ref[:, cols]
        o_pair = jnp.zeros((t, lanes), F32)
        for slot in range(HEADS_PER_TILE):
            head = _head_of(pair, slot)
            shift = slot * HEAD_DIM
            to_low = ((sq_row == sq_col + shift) & (sq_col < HEAD_DIM)).astype(BF16)
            to_slot = ((sq_col == sq_row + shift) & (sq_row < HEAD_DIM)).astype(BF16)
            low = lambda a: jnp.dot(a, to_low, preferred_element_type=F32)
            neg_parts = [-part[:, head:head + 1] for part in key_parts]
            k_sc[0:tk, :] = _augment(None, lane, HEAD_DIM, None, False, neg_parts).astype(BF16)
            v_sc[0:tk, :] = ones_col
            k_sc[0:past, 0:HEAD_DIM] = kh_ref[:, head, :].astype(BF16)
            v_sc[0:past, 0:HEAD_DIM] = vh_ref[:, head, :].astype(BF16)
            k_sc[past:tk, 0:HEAD_DIM] = low(kn_pair)[:, 0:HEAD_DIM].astype(BF16)
            v_sc[past:tk, 0:HEAD_DIM] = low(vn_pair)[:, 0:HEAD_DIM].astype(BF16)
            q_wide = _augment(lane_q < HEAD_DIM, lane_q, HEAD_DIM, low(q_pair), True,
                              [part[:, head:head + 1] for part in query_parts]).astype(BF16)
            nt = (((1,), (1,)), ((), ()))
            s_hist = lax.dot_general(q_wide, k_sc[0:past, :], nt, preferred_element_type=F32)
            s_new = lax.dot_general(q_wide, k_sc[past:past + lanes, :], nt, preferred_element_type=F32)
            s_new = jnp.where(causal, s_new, -jnp.inf)
            m = jnp.maximum(jnp.max(s_hist, axis=1, keepdims=True), jnp.max(s_new, axis=1, keepdims=True))
            acc = (jnp.dot(jnp.exp2(s_hist - m).astype(BF16), v_sc[0:past, :], preferred_element_type=F32)
                   + jnp.dot(jnp.exp2(s_new - m).astype(BF16), v_sc[past:past + lanes, :],
                             preferred_element_type=F32))
            o_head = (acc / acc[:, HEAD_DIM:HEAD_DIM + 1]).astype(BF16)
            o_pair = o_pair + jnp.dot(o_head, to_slot, preferred_element_type=F32)
        o_ref[:, cols] = o_pair.astype(o_ref.dtype)


def _attention_hist(q, k_new, v_new, k_hist, v_hist, c):
    b, t, d = q.shape
    past = k_hist.shape[1]
    lanes = V7X_LANES
    assert t <= lanes and t % V7X_SUBLANES == 0 and past % lanes == 0
    new_spec = pl.BlockSpec((None, t, d), lambda i: (i, 0, 0))
    hist_spec = pl.BlockSpec((None, past, N_HEADS, HEAD_DIM), lambda i: (i, 0, 0, 0))
    vmem = (2 * 2 * past * N_HEADS * lanes * 4 + 8 * t * d * 2 + 2 * (past + t) * lanes * 4
            + 2 * (past + lanes) * lanes * 2 + 16 * (past + t) * lanes * 4)
    return pl.pallas_call(
        functools.partial(_attn_hist_kernel, past=past),
        grid=(b,),
        in_specs=[new_spec, new_spec, new_spec, hist_spec, hist_spec,
                  pl.BlockSpec((None, past + t, N_HEADS), lambda i: (i, 0, 0))],
        out_specs=new_spec,
        out_shape=jax.ShapeDtypeStruct((b, t, d), BF16),
        scratch_shapes=[pltpu.VMEM((past + lanes, lanes), BF16), pltpu.VMEM((past + lanes, lanes), BF16)],
        compiler_params=_params(("arbitrary",), vmem),
        name="fox_attention_hist",
    )(q, k_new, v_new, k_hist, v_hist, c)


def _neg_expm1(y, exp_half_y):
    series = y * (-1.0 / 120.0) - 1.0 / 24.0
    for coeff in (-1.0 / 6.0, -0.5, -1.0):
        series = series * y + coeff
    return jnp.where(y > -0.0625, y * series, 1.0 - exp_half_y * exp_half_y)


def _scan_groups(a, b):
    rows, width = a.shape
    sub = V7X_SUBLANES
    a = a.reshape(rows // sub, sub, width)
    b = b.reshape(rows // sub, sub, width)
    row = lax.broadcasted_iota(jnp.int32, a.shape, 1)
    for shift in (1, 2, 4):
        keep = row >= shift
        a_prev = jnp.where(keep, pltpu.roll(a, shift, 1), 1.0)
        b_prev = jnp.where(keep, pltpu.roll(b, shift, 1), 0.0)
        b = a * b_prev + b
        a = a * a_prev
    return a.reshape(rows, width), b.reshape(rows, width)


def _rglru_kernel(*refs, tt, has_state):
    if has_state:
        (xr_ref, yr_ref, ch_ref, h0_ref, cw_ref, cb_ref, wg_ref, ba_ref, bx_ref, lam_ref,
         o_ref, hl_ref, ext_sc, a_sc, b_sc, h_sc, hc_sc) = refs
    else:
        (xr_ref, yr_ref, cw_ref, cb_ref, wg_ref, ba_ref, bx_ref, lam_ref,
         o_ref, hl_ref, ext_sc, a_sc, b_sc, h_sc, hc_sc) = refs
    sub = V7X_SUBLANES
    d = xr_ref.shape[1]

    @pl.when(pl.program_id(1) == 0)
    def _init_state():
        if has_state:
            ext_sc[0:sub, :] = ch_ref[...]
            hc_sc[...] = h0_ref[...]
        else:
            ext_sc[0:sub, :] = jnp.zeros((sub, d), F32)
            hc_sc[...] = jnp.zeros((1, d), F32)

    ext_sc[sub:sub + tt, :] = xr_ref[...].astype(F32)
    cw = cw_ref[...]
    xc = ext_sc[sub - 3:sub - 3 + tt, :] * cw[0:1, :]
    for w in range(1, CONV_W):
        xc = xc + ext_sc[sub - 3 + w:sub - 3 + w + tt, :] * cw[w:w + 1, :]
    xc = cb_ref[...] + xc
    ext_sc[0:sub, :] = ext_sc[tt:tt + sub, :]

    xcb = xc.astype(BF16)
    wb = V7X_MXU_DIM
    for j in range(d // wb):
        cols = slice(j * wb, (j + 1) * wb)
        xcj = xc[:, cols]
        g = jnp.dot(xcb[:, cols], wg_ref[j], preferred_element_type=F32)
        r = jax.nn.sigmoid(g[:, :wb] + ba_ref[:, cols])
        i = jax.nn.sigmoid(g[:, wb:] + bx_ref[:, cols])
        log_a = (-RG_C * r) * _softplus(-lam_ref[:, cols])
        a = jnp.exp(log_a)
        bx = jnp.sqrt(_neg_expm1(2.0 * log_a, a)) * (i * xcj)
        a_sc[:, cols], b_sc[:, cols] = _scan_groups(a, bx)

    hc = hc_sc[...]
    for grp in range(tt // sub):
        rows = slice(grp * sub, (grp + 1) * sub)
        h = a_sc[rows, :] * hc + b_sc[rows, :]
        h_sc[rows, :] = h
        hc = h[sub - 1:sub, :]
    hc_sc[...] = hc
    hl_ref[...] = hc
    o_ref[...] = (h_sc[...] * _gelu_tanh(yr_ref[...].astype(F32))).astype(o_ref.dtype)


def _rglru(xr, yr, conv_hist, h0, cw, cb, wg, ba, bx, lam):
    b, t, d = xr.shape
    tt = min(RNN_TIME_TILE, t)
    assert t % tt == 0 and tt % V7X_SUBLANES == 0
    has_state = conv_hist is not None
    tok_spec = pl.BlockSpec((None, tt, d), lambda i, j: (i, j, 0))
    in_specs = [tok_spec, tok_spec]
    args = [xr, yr]
    if has_state:
        in_specs += [pl.BlockSpec((None, V7X_SUBLANES, d), lambda i, j: (i, 0, 0)),
                     pl.BlockSpec((None, 1, d), lambda i, j: (i, 0, 0))]
        args += [conv_hist, h0]
    weights = [cw, cb, wg, ba, bx, lam]
    in_specs += [_const_spec(w.shape) for w in weights]
    args += weights
    vmem = 4 * tt * d * 2 * 2 + 2 * tt * d * 2 + 4 * (tt + 8) * d * 4 + 16 * tt * d * 4
    return pl.pallas_call(
        functools.partial(_rglru_kernel, tt=tt, has_state=has_state),
        grid=(b, t // tt),
        in_specs=in_specs,
        out_specs=(tok_spec, pl.BlockSpec((None, 1, d), lambda i, j: (i, 0, 0))),
        out_shape=(jax.ShapeDtypeStruct((b, t, d), BF16), jax.ShapeDtypeStruct((b, 1, d), F32)),
        scratch_shapes=[pltpu.VMEM((tt + V7X_SUBLANES, d), F32), pltpu.VMEM((tt, d), F32),
                        pltpu.VMEM((tt, d), F32), pltpu.VMEM((tt, d), F32), pltpu.VMEM((1, d), F32)],
        compiler_params=_params(("arbitrary", "arbitrary"), vmem),
        name="rglru",
    )(*args)


def _out_mlp_kernel(x_ref, oa_ref, or_ref, ga_ref, gb_ref, wpa_ref, wpr_ref, wo_ref, g2_ref,
                    wup_ref, wdn_ref, gf_ref, y_ref):
    d = x_ref.shape[1]
    ya = jnp.dot(oa_ref[...], wpa_ref[...], preferred_element_type=F32)
    yb = jnp.dot(or_ref[...], wpr_ref[...], preferred_element_type=F32)
    merged = (jax.nn.sigmoid(ga_ref[...].astype(F32)) * ya
              + jax.nn.sigmoid(gb_ref[...].astype(F32)) * yb)
    x1 = x_ref[...] + jnp.dot(merged.astype(BF16), wo_ref[...], preferred_element_type=F32)
    h2 = _rmsnorm(x1, g2_ref[...]).astype(BF16)
    acc = x1
    for j in range(wup_ref.shape[1] // d):
        cols = slice(j * d, (j + 1) * d)
        u = jnp.maximum(jnp.dot(h2, wup_ref[:, cols], preferred_element_type=F32), 0.0)
        acc = acc + jnp.dot((u * u).astype(BF16), wdn_ref[cols, :], preferred_element_type=F32)
    y_ref[...] = _rmsnorm(acc, gf_ref[...])


def _out_mlp(x2d, oa, orn, ga, gb, wpa, wpr, wo, g2, wup, wdn, gf):
    n, d = x2d.shape
    tm = min(TOKEN_TILE, n)
    assert n % tm == 0
    tok_spec = pl.BlockSpec((tm, d), lambda i: (i, 0))
    weights = [wpa, wpr, wo, g2, wup, wdn, gf]
    vmem = (sum(w.size * w.dtype.itemsize for w in weights) + 2 * tm * d * (4 + 4 * 2 + 4)
            + 10 * tm * d * 4)
    return pl.pallas_call(
        _out_mlp_kernel,
        grid=(n // tm,),
        in_specs=[tok_spec] * 5 + [_const_spec(w.shape) for w in weights],
        out_specs=tok_spec,
        out_shape=jax.ShapeDtypeStruct((n, d), F32),
        compiler_params=_params(("arbitrary",), vmem),
        name="out_mlp",
    )(x2d, oa, orn, ga, gb, *weights)


def _prepare_weights(norm_mix_g, w_in, b_f, conv_w, conv_b, w_rg_a, b_rg_a, w_rg_x, b_rg_x, rg_lambda,
                     w_proj_attn, w_proj_rnn, w_out, norm_mlp_g, w_up, w_down, norm_final_g):
    d = w_in.shape[0]
    d_attn = N_HEADS * HEAD_DIM
    qkv_end = 3 * d_attn
    rest = qkv_end + N_HEADS
    order = np.array([_head_of(p, s) * HEAD_DIM + f for p in range(N_PAIRS)
                      for s in range(HEADS_PER_TILE) for f in range(HEAD_DIM)])
    w_q = w_in[:, :d_attn] * (HEAD_DIM ** -0.5 * LOG2_E)
    w_qkv = jnp.stack([w[:, order] for w in (w_q, w_in[:, d_attn:2 * d_attn],
                                             w_in[:, 2 * d_attn:qkv_end])]).astype(BF16)
    w_gate = jnp.stack([w_in[:, rest + s * d:rest + (s + 1) * d] for s in range(4)]).astype(BF16)
    pad = V7X_LANES - N_HEADS
    wf = jnp.pad(w_in[:, qkv_end:rest], ((0, 0), (0, pad))).astype(BF16)
    bf = jnp.pad(b_f, (0, pad)).reshape(1, V7X_LANES)

    def block_diag(w):
        n_tiles = w.shape[0] // GATE_BLOCKS_PER_TILE
        w4 = w.reshape(n_tiles, GATE_BLOCKS_PER_TILE, RNN_BLOCK, RNN_BLOCK)
        eye = jnp.eye(GATE_BLOCKS_PER_TILE, dtype=w.dtype)
        return jnp.einsum('jmde,mn->jmdne', w4, eye).reshape(n_tiles, V7X_MXU_DIM, V7X_MXU_DIM)

    wg = jnp.concatenate([block_diag(w_rg_a), block_diag(w_rg_x)], axis=-1).astype(BF16)
    row = lambda v: v.reshape(1, -1)
    return dict(
        qkv_proj=(row(norm_mix_g), w_qkv),
        gate_proj=(row(norm_mix_g), w_gate, wf, bf),
        rglru=(conv_w, row(conv_b), wg, row(b_rg_a), row(b_rg_x), row(rg_lambda)),
        out_mlp=(w_proj_attn[order, :].astype(BF16), w_proj_rnn.astype(BF16), w_out.astype(BF16),
                 row(norm_mlp_g), w_up.astype(BF16), w_down.astype(BF16), row(norm_final_g)))


def _layer(x, k_hist, v_hist, logf_hist, conv_hist, h0, wts):
    b, t, d = x.shape
    x2d = x.reshape(b * t, d)
    q, kb, vb, k4, v4 = _qkv_proj(x2d, *wts['qkv_proj'])
    xr, yr, ga, gb, lf, tail = _gate_proj(x2d, *wts['gate_proj'], seq_len=t)
    as_seq = lambda a: a.reshape(b, t, d)
    lf_new = lf.reshape(b, t, N_HEADS)
    if k_hist is None:
        o_attn = _attention_first(as_seq(q), as_seq(kb), as_seq(vb), _cumsum(lf_new))
    else:
        c = _cumsum(jnp.concatenate([logf_hist, lf_new], axis=1))
        o_attn = _attention_hist(as_seq(q), as_seq(kb), as_seq(vb), k_hist, v_hist, c)
        conv_hist = jnp.pad(conv_hist, ((0, 0), (V7X_SUBLANES - (CONV_W - 1), 0), (0, 0)))
        h0 = h0.reshape(b, 1, d)
    o_rnn, h_last = _rglru(as_seq(xr), as_seq(yr), conv_hist, h0, *wts['rglru'])
    y = _out_mlp(x2d, o_attn.reshape(b * t, d), o_rnn.reshape(b * t, d), ga, gb, *wts['out_mlp'])
    new_k = k4.reshape(b, t, N_HEADS, HEAD_DIM)
    new_v = v4.reshape(b, t, N_HEADS, HEAD_DIM)
    new_conv = tail[:, V7X_SUBLANES - (CONV_W - 1):, :]
    return y.reshape(b, t, d), new_k, new_v, lf_new, new_conv, h_last.reshape(b, d)


def kernel(x_prompt, x_sample, cache_k, cache_v, cache_logf, state_conv, state_rglru, norm_mix_g, w_in, b_f, conv_w, conv_b, w_rg_a, b_rg_a, w_rg_x, b_rg_x, rg_lambda, w_proj_attn, w_proj_rnn, w_out, norm_mlp_g, w_up, w_down, norm_final_g):
    assert w_in.shape[0] == 1, "the final norm is fused into the layer: one layer only"
    wts = _prepare_weights(norm_mix_g[0], w_in[0], b_f[0], conv_w[0], conv_b[0], w_rg_a[0], b_rg_a[0],
                           w_rg_x[0], b_rg_x[0], rg_lambda[0], w_proj_attn[0], w_proj_rnn[0], w_out[0],
                           norm_mlp_g[0], w_up[0], w_down[0], norm_final_g)
    yp, kp, vp, lp, cp, hp = _layer(x_prompt, None, None, None, None, None, wts)
    ys, ks, vs, ls, cs, hs = _layer(x_sample, cache_k[0], cache_v[0], cache_logf[0], state_conv[0],
                                    state_rglru[0], wts)
    lead = lambda a: a[None]
    return (yp, ys, lead(kp), lead(vp), lead(lp), lead(cp), lead(hp),
            lead(ks), lead(vs), lead(ls), lead(cs), lead(hs))
```

```python
import functools

import numpy as np
import jax
import jax.numpy as jnp
from jax import lax
from jax.experimental import pallas as pl
from jax.experimental.pallas import tpu as pltpu

F32 = jnp.float32
BF16 = jnp.bfloat16

N_HEADS = 16
HEAD_DIM = 64
CONV_W = 4
RNN_BLOCK = 64
RG_C = 8.0
EPS = 1e-6
LOG2_E = 1.4426950408889634

V7X_LANES = 128
V7X_SUBLANES = 8
V7X_MXU_DIM = 256
V7X_VMEM_BYTES = 64 * 1024 * 1024

HEADS_PER_TILE = V7X_LANES // HEAD_DIM
N_PAIRS = N_HEADS // HEADS_PER_TILE
GATE_BLOCKS_PER_TILE = V7X_MXU_DIM // RNN_BLOCK
TOKEN_TILE = 512
ATTN_Q_TILE = 512
RNN_TIME_TILE = 256
CUMSUM_BLOCK = 256
N_SPLIT = 3


HEADS_NATURAL = (HEADS_PER_TILE, 1)
HEADS_SPLIT = (1, N_PAIRS)


def _head_of(order, pair, slot):
    return pair * order[0] + slot * order[1]


def _vmem_limit(estimate_bytes):
    return int(min(estimate_bytes + (12 << 20), V7X_VMEM_BYTES - (6 << 20)))


def _params(semantics, vmem_estimate):
    return pltpu.CompilerParams(dimension_semantics=semantics,
                                vmem_limit_bytes=_vmem_limit(vmem_estimate))


def _const_spec(shape):
    zeros = (0,) * len(shape)
    return pl.BlockSpec(shape, lambda *_: zeros, pipeline_mode=pl.Buffered(1))


def _rmsnorm(x, g):
    x = x * lax.rsqrt(jnp.mean(x * x, axis=-1, keepdims=True) + EPS)
    return x * g


def _log_sigmoid(x):
    return jnp.minimum(x, 0.0) - jnp.log1p(jnp.exp(-jnp.abs(x)))


def _softplus(x):
    return jnp.maximum(x, 0.0) + jnp.log1p(jnp.exp(-jnp.abs(x)))


def _gelu_tanh(x):
    c = 0.7978845608028654
    return x * (0.5 * (1.0 + jnp.tanh(c * (x + 0.044715 * (x * x * x)))))


def _split3(x):
    hi = x.astype(BF16).astype(F32)
    r1 = x - hi
    mid = r1.astype(BF16).astype(F32)
    lo = (r1 - mid).astype(BF16).astype(F32)
    return hi, mid, lo


def _store_head_major(r, o_ref, slab_sc):
    tm = r.shape[0]
    sub = V7X_SUBLANES
    n_tiles = r.shape[1] // V7X_LANES
    for j in range(n_tiles):
        slab_sc[:, j * sub:(j + 1) * sub, :] = (
            r[:, j * V7X_LANES:(j + 1) * V7X_LANES].reshape(tm // sub, sub, V7X_LANES))
    for t in range(sub):
        rows = slab_sc[:, pl.ds(t, n_tiles, stride=sub), :]
        o_ref[:, t, 0:N_PAIRS, :] = rows[:, :, 0:HEAD_DIM]
        o_ref[:, t, N_PAIRS:N_HEADS, :] = rows[:, :, HEAD_DIM:V7X_LANES]


def _qkv_proj_kernel(x_ref, g_ref, w_ref, q_ref, kb_ref, vb_ref, ks_ref, vs_ref, *scratch, position_minor):
    hn = _rmsnorm(x_ref[...], g_ref[...]).astype(BF16)
    q_ref[...] = jnp.dot(hn, w_ref[0], preferred_element_type=F32).astype(q_ref.dtype)
    for s, (b_ref, state_ref) in enumerate(((kb_ref, ks_ref), (vb_ref, vs_ref)), start=1):
        r = jnp.dot(hn, w_ref[s], preferred_element_type=F32)
        b_ref[...] = r.astype(b_ref.dtype)
        if position_minor:
            state_ref[...] = r.T.reshape(state_ref.shape)
        else:
            _store_head_major(r, state_ref, *scratch)


def _qkv_proj(x2d, g, w_qkv, seq_len, position_minor):
    n, d = x2d.shape
    tm = min(TOKEN_TILE, n)
    sub = V7X_SUBLANES
    assert n % tm == 0 and tm % sub == 0 and d == N_HEADS * HEAD_DIM
    tok_spec = pl.BlockSpec((tm, d), lambda i: (i, 0))
    tok_bf16 = jax.ShapeDtypeStruct((n, d), BF16)
    if position_minor:
        assert seq_len % tm == 0
        tiles = seq_len // tm
        state_spec = pl.BlockSpec((None, N_HEADS, HEAD_DIM, tm), lambda i: (i // tiles, 0, 0, i % tiles))
        state_f32 = jax.ShapeDtypeStruct((n // seq_len, N_HEADS, HEAD_DIM, seq_len), F32)
        scratch, state_bytes = [], tm * d * 4
    else:
        state_spec = pl.BlockSpec((tm // sub, sub, N_HEADS, HEAD_DIM), lambda i: (i, 0, 0, 0))
        state_f32 = jax.ShapeDtypeStruct((n // sub, sub, N_HEADS, HEAD_DIM), F32)
        scratch = [pltpu.VMEM((tm // sub, d // V7X_LANES * sub, V7X_LANES), F32)]
        state_bytes = tm * N_HEADS * V7X_LANES * 4
    vmem = w_qkv.size * 2 + 2 * tm * d * 4 + 2 * 3 * tm * d * 2 + 2 * 2 * state_bytes + 6 * tm * d * 4
    return pl.pallas_call(
        functools.partial(_qkv_proj_kernel, position_minor=position_minor),
        grid=(n // tm,),
        in_specs=[tok_spec, _const_spec(g.shape), _const_spec(w_qkv.shape)],
        out_specs=(tok_spec, tok_spec, tok_spec, state_spec, state_spec),
        out_shape=(tok_bf16, tok_bf16, tok_bf16, state_f32, state_f32),
        scratch_shapes=scratch,
        compiler_params=_params(("arbitrary",), vmem),
        name="qkv_proj",
    )(x2d, g, w_qkv)


def _gate_proj_kernel(x_ref, g_ref, w_ref, wf_ref, bf_ref,
                      xr_ref, yr_ref, ga_ref, gb_ref, lf_ref, tail_ref, *, seg_len):
    hn = _rmsnorm(x_ref[...], g_ref[...]).astype(BF16)
    tm = hn.shape[0]
    for s, o_ref in enumerate((xr_ref, yr_ref, ga_ref, gb_ref)):
        r = jnp.dot(hn, w_ref[s], preferred_element_type=F32)
        o_ref[...] = r.astype(o_ref.dtype)
        if o_ref is xr_ref:
            for j in range(tm // seg_len):
                end = (j + 1) * seg_len
                tail_ref[j] = r[end - V7X_SUBLANES:end, :]
    zf = jnp.dot(hn, wf_ref[...], preferred_element_type=F32)
    lf_ref[...] = _log_sigmoid(zf + bf_ref[...])[:, :N_HEADS]


def _gate_proj(x2d, g, w_gate, wf, bf, seq_len):
    n, d = x2d.shape
    tm = min(TOKEN_TILE, n)
    assert n % tm == 0 and (seq_len % tm == 0 or tm % seq_len == 0)
    seg_len = min(seq_len, tm)
    nseg = tm // seg_len
    tiles_per_stream = max(1, seq_len // tm)
    row = lambda i: (i, 0)
    tok_bf16 = jax.ShapeDtypeStruct((n, d), BF16)
    tok_spec = pl.BlockSpec((tm, d), row)
    out_shape = (tok_bf16,) * 4 + (jax.ShapeDtypeStruct((n, N_HEADS), F32),
                                   jax.ShapeDtypeStruct((n // seq_len, V7X_SUBLANES, d), F32))
    out_specs = (tok_spec,) * 4 + (
        pl.BlockSpec((tm, N_HEADS), row),
        pl.BlockSpec((nseg, V7X_SUBLANES, d), lambda i: (i // tiles_per_stream, 0, 0)))
    vmem = (w_gate.size + wf.size) * 2 + 2 * tm * d * 4 + 2 * 4 * tm * d * 2 + 5 * tm * d * 4
    return pl.pallas_call(
        functools.partial(_gate_proj_kernel, seg_len=seg_len),
        grid=(n // tm,),
        in_specs=[tok_spec, _const_spec(g.shape), _const_spec(w_gate.shape),
                  _const_spec(wf.shape), _const_spec(bf.shape)],
        out_specs=out_specs,
        out_shape=out_shape,
        compiler_params=_params(("arbitrary",), vmem),
        name="gate_proj",
    )(x2d, g, w_gate, wf, bf)


def _cumsum_kernel(lf_ref, c_ref):
    tk = lf_ref.shape[0]
    blk = min(CUMSUM_BLOCK, tk)
    row = lax.broadcasted_iota(jnp.int32, (blk, blk), 0)
    col = lax.broadcasted_iota(jnp.int32, (blk, blk), 1)
    lower = (row >= col).astype(BF16)
    carry = jnp.zeros((1, N_HEADS), F32)
    for s in range(0, tk, blk):
        sz = min(blk, tk - s)
        c = carry
        for part in _split3(lf_ref[s:s + sz, :]):
            c = c + jnp.dot(lower[:sz, :sz], part.astype(BF16), preferred_element_type=F32)
        c_ref[s:s + sz, :] = c * LOG2_E
        carry = c[sz - 1:sz, :]


def _cumsum(lf):
    b, tk, h = lf.shape
    spec = pl.BlockSpec((None, tk, h), lambda i: (i, 0, 0))
    return pl.pallas_call(
        _cumsum_kernel,
        grid=(b,),
        in_specs=[spec],
        out_specs=spec,
        out_shape=jax.ShapeDtypeStruct((b, tk, h), F32),
        compiler_params=_params(("arbitrary",), 8 * tk * V7X_LANES * 4),
        name="logf_cumsum",
    )(lf)


def _augment(own, lane, base, data, ones_first, parts):
    aug = jnp.zeros(lane.shape, F32)
    for i, part in enumerate(parts):
        at = base + i + (N_SPLIT if ones_first else 0)
        aug = jnp.where(lane == at, part, aug)
    ones_at = base if ones_first else base + N_SPLIT
    aug = jnp.where((lane >= ones_at) & (lane < ones_at + N_SPLIT), 1.0, aug)
    return aug if data is None else jnp.where(own, data.astype(F32), aug)


def _attn_first_kernel(q_ref, k_ref, v_ref, c_ref, o_ref, kt_sc, v_sc, *, tq):
    pair = pl.program_id(1)
    qi = pl.program_id(2)
    n_chunks = k_ref.shape[0] // tq
    lane = lax.broadcasted_iota(jnp.int32, (tq, V7X_LANES), 1)
    own = (lane < HEAD_DIM, lane >= HEAD_DIM)
    base = (HEAD_DIM, 0)

    def head_columns(c_rows):
        head = lax.broadcasted_iota(jnp.int32, c_rows.shape, 1)
        return [jnp.sum(jnp.where(head == _head_of(HEADS_NATURAL, pair, h), c_rows, 0.0), axis=1, keepdims=True)
                for h in range(HEADS_PER_TILE)]

    def stage_keys():
        for j in range(n_chunks):
            rows = slice(j * tq, (j + 1) * tq)
            k = k_ref[rows, :]
            v = v_ref[rows, :]
            c_cols = head_columns(c_ref[rows, :])
            for h in range(HEADS_PER_TILE):
                neg_parts = [-part for part in _split3(c_cols[h])]
                k_wide = _augment(own[h], lane, base[h], k, False, neg_parts)
                kt_sc[h, j] = k_wide.T.astype(BF16)
                v_sc[h, rows, :] = jnp.where(own[h], v.astype(F32),
                                             jnp.where(lane == base[h], 1.0, 0.0)).astype(BF16)

    causal = (lax.broadcasted_iota(jnp.int32, (tq, tq), 1)
              <= lax.broadcasted_iota(jnp.int32, (tq, tq), 0))

    def query_tile(n):
        if n == 0:
            stage_keys()
        q = q_ref[...]
        cq_cols = head_columns(c_ref[n * tq:(n + 1) * tq, :])
        for h in range(HEADS_PER_TILE):
            q_wide = _augment(own[h], lane, base[h], q, True, _split3(cq_cols[h])).astype(BF16)
            s = jnp.dot(q_wide, kt_sc[h, n], preferred_element_type=F32)
            s = jnp.where(causal, s, -jnp.inf)
            m = jnp.max(s, axis=1, keepdims=True)
            p = jnp.exp2(s - m).astype(BF16)
            acc = jnp.dot(p, v_sc[h, n * tq:(n + 1) * tq, :], preferred_element_type=F32)
            for j in range(n):
                s = jnp.dot(q_wide, kt_sc[h, j], preferred_element_type=F32)
                m_new = jnp.maximum(m, jnp.max(s, axis=1, keepdims=True))
                p = jnp.exp2(s - m_new).astype(BF16)
                acc = (jnp.exp2(m - m_new) * acc
                       + jnp.dot(p, v_sc[h, j * tq:(j + 1) * tq, :], preferred_element_type=F32))
                m = m_new
            out = acc / acc[:, base[h]:base[h] + 1]
            lanes_h = slice(h * HEAD_DIM, (h + 1) * HEAD_DIM)
            o_ref[:, lanes_h] = out[:, lanes_h].astype(o_ref.dtype)

    for n in range(n_chunks):
        pl.when(qi == n)(functools.partial(query_tile, n))


def _attention_first(q, k, v, c):
    b, t, d = q.shape
    tq = min(ATTN_Q_TILE, t)
    assert t % tq == 0
    lanes = V7X_LANES
    seq_spec = pl.BlockSpec((None, t, lanes), lambda i, p, j: (i, 0, p))
    q_spec = pl.BlockSpec((None, tq, lanes), lambda i, p, j: (i, j, p))
    vmem = (4 * t * lanes * 2 + 2 * t * lanes * 4 + 4 * t * lanes * 2 + 10 * tq * tq * 4
            + 8 * tq * lanes * 4)
    return pl.pallas_call(
        functools.partial(_attn_first_kernel, tq=tq),
        grid=(b, N_PAIRS, t // tq),
        in_specs=[q_spec, seq_spec, seq_spec, pl.BlockSpec((None, t, N_HEADS), lambda i, p, j: (i, 0, 0))],
        out_specs=q_spec,
        out_shape=jax.ShapeDtypeStruct((b, t, d), BF16),
        scratch_shapes=[pltpu.VMEM((HEADS_PER_TILE, t // tq, lanes, tq), BF16),
                        pltpu.VMEM((HEADS_PER_TILE, t, lanes), BF16)],
        compiler_params=_params(("arbitrary", "arbitrary", "arbitrary"), vmem),
        name="fox_attention_first",
    )(q, k, v, c)


def _attn_hist_kernel(q_ref, kn_ref, vn_ref, kh_ref, vh_ref, c_ref, o_ref, k_sc, v_sc, *, past):
    t = q_ref.shape[0]
    tk = past + t
    lanes = V7X_LANES
    pad_rows = k_sc.shape[0] - tk
    lane = lax.broadcasted_iota(jnp.int32, (tk, lanes), 1)
    sq_row = lax.broadcasted_iota(jnp.int32, (lanes, lanes), 0)
    sq_col = lax.broadcasted_iota(jnp.int32, (lanes, lanes), 1)
    causal = (lax.broadcasted_iota(jnp.int32, (t, lanes), 1)
              <= lax.broadcasted_iota(jnp.int32, (t, lanes), 0))
    key_parts = _split3(c_ref[...])
    query_parts = _split3(c_ref[past:tk, :])
    lane_q = lax.broadcasted_iota(jnp.int32, (t, lanes), 1)
    ones_col =jnp.where(lane == HEAD_DIM, 1.0, 0.0).astype(BF16)
    zero_pad = jnp.zeros((pad_rows, lanes), BF16)
    k_sc[tk:tk + pad_rows, :] = zero_pad
    v_sc[tk:tk + pad_rows, :] = zero_pad

    for pair in range(N_PAIRS):
        cols = slice(pair * lanes, (pair + 1) * lanes)
        q_pair, kn_pair, vn_pair = q_ref[:, cols], kn_ref[:, cols], vn_ref[:, cols]
        o_pair = jnp.zeros((t, lanes), F32)
        for slot in range(HEADS_PER_TILE):
            head = _head_of(HEADS_SPLIT, pair, slot)
            shift = slot * HEAD_DIM
            to_low = ((sq_row == sq_col + shift) & (sq_col < HEAD_DIM)).astype(BF16)
            to_slot = ((sq_col == sq_row + shift) & (sq_row < HEAD_DIM)).astype(BF16)
            low = lambda a: jnp.dot(a, to_low, preferred_element_type=F32)
            neg_parts = [-part[:, head:head + 1] for part in key_parts]
            k_sc[0:tk, :] = _augment(None, lane, HEAD_DIM, None, False, neg_parts).astype(BF16)
            v_sc[0:tk, :] = ones_col
            k_sc[0:past, 0:HEAD_DIM] = kh_ref[pl.ds(head, past, stride=N_HEADS), :].astype(BF16)
            v_sc[0:past, 0:HEAD_DIM] = vh_ref[pl.ds(head, past, stride=N_HEADS), :].astype(BF16)
            k_sc[past:tk, 0:HEAD_DIM] = low(kn_pair)[:, 0:HEAD_DIM].astype(BF16)
            v_sc[past:tk, 0:HEAD_DIM] = low(vn_pair)[:, 0:HEAD_DIM].astype(BF16)
            q_wide = _augment(lane_q < HEAD_DIM, lane_q, HEAD_DIM, low(q_pair), True,
                              [part[:, head:head + 1] for part in query_parts]).astype(BF16)
            nt = (((1,), (1,)), ((), ()))
            s_hist = lax.dot_general(q_wide, k_sc[0:past, :], nt, preferred_element_type=F32)
            s_new = lax.dot_general(q_wide, k_sc[past:past + lanes, :], nt, preferred_element_type=F32)
            s_new = jnp.where(causal, s_new, -jnp.inf)
            m = jnp.maximum(jnp.max(s_hist, axis=1, keepdims=True), jnp.max(s_new, axis=1, keepdims=True))
            acc = (jnp.dot(jnp.exp2(s_hist - m).astype(BF16), v_sc[0:past, :], preferred_element_type=F32)
                   + jnp.dot(jnp.exp2(s_new - m).astype(BF16), v_sc[past:past + lanes, :],
                             preferred_element_type=F32))
            o_head = (acc / acc[:, HEAD_DIM:HEAD_DIM + 1]).astype(BF16)
            o_pair = o_pair + jnp.dot(o_head, to_slot, preferred_element_type=F32)
        o_ref[:, cols] = o_pair.astype(o_ref.dtype)


def _attention_hist(q, k_new, v_new, k_hist, v_hist, c):
    b, t, d = q.shape
    past = k_hist.shape[1]
    lanes = V7X_LANES
    assert t <= lanes and t % V7X_SUBLANES == 0 and past % lanes == 0
    k_rows = k_hist.reshape(b, past * N_HEADS, HEAD_DIM)
    v_rows = v_hist.reshape(b, past * N_HEADS, HEAD_DIM)
    new_spec = pl.BlockSpec((None, t, d), lambda i: (i, 0, 0))
    hist_spec = pl.BlockSpec((None, past * N_HEADS, HEAD_DIM), lambda i: (i, 0, 0))
    vmem = (2 * 2 * past * N_HEADS * lanes * 4 + 8 * t * d * 2 + 2 * (past + t) * lanes * 4
            + 2 * (past + lanes) * lanes * 2 + 16 * (past + t) * lanes * 4)
    return pl.pallas_call(
        functools.partial(_attn_hist_kernel, past=past),
        grid=(b,),
        in_specs=[new_spec, new_spec, new_spec, hist_spec, hist_spec,
                  pl.BlockSpec((None, past + t, N_HEADS), lambda i: (i, 0, 0))],
        out_specs=new_spec,
        out_shape=jax.ShapeDtypeStruct((b, t, d), BF16),
        scratch_shapes=[pltpu.VMEM((past + lanes, lanes), BF16), pltpu.VMEM((past + lanes, lanes), BF16)],
        compiler_params=_params(("arbitrary",), vmem),
        name="fox_attention_hist",
    )(q, k_new, v_new, k_rows, v_rows, c)


def _neg_expm1(y, exp_half_y):
    series = y * (-1.0 / 120.0) - 1.0 / 24.0
    for coeff in (-1.0 / 6.0, -0.5, -1.0):
        series = series * y + coeff
    return jnp.where(y > -0.0625, y * series, 1.0 - exp_half_y * exp_half_y)


def _scan_groups(a, b):
    rows, width = a.shape
    sub = V7X_SUBLANES
    a = a.reshape(rows // sub, sub, width)
    b = b.reshape(rows // sub, sub, width)
    row = lax.broadcasted_iota(jnp.int32, a.shape, 1)
    for shift in (1, 2, 4):
        keep = row >= shift
        a_prev = jnp.where(keep, pltpu.roll(a, shift, 1), 1.0)
        b_prev = jnp.where(keep, pltpu.roll(b, shift, 1), 0.0)
        b = a * b_prev + b
        a = a * a_prev
    return a.reshape(rows, width), b.reshape(rows, width)


def _rglru_kernel(*refs, tt, has_state):
    if has_state:
        (xr_ref, yr_ref, ch_ref, h0_ref, cw_ref, cb_ref, wg_ref, ba_ref, bx_ref, lam_ref,
         o_ref, hl_ref, ext_sc, a_sc, b_sc, h_sc, hc_sc) = refs
    else:
        (xr_ref, yr_ref, cw_ref, cb_ref, wg_ref, ba_ref, bx_ref, lam_ref,
         o_ref, hl_ref, ext_sc, a_sc, b_sc, h_sc, hc_sc) = refs
    sub = V7X_SUBLANES
    d = xr_ref.shape[1]

    @pl.when(pl.program_id(1) == 0)
    def _init_state():
        if has_state:
            ext_sc[0:sub, :] = ch_ref[...]
            hc_sc[...] = h0_ref[...]
        else:
            ext_sc[0:sub, :] = jnp.zeros((sub, d), F32)
            hc_sc[...] = jnp.zeros((1, d), F32)

    ext_sc[sub:sub + tt, :] = xr_ref[...].astype(F32)
    cw = cw_ref[...]
    xc = ext_sc[sub - 3:sub - 3 + tt, :] * cw[0:1, :]
    for w in range(1, CONV_W):
        xc = xc + ext_sc[sub - 3 + w:sub - 3 + w + tt, :] * cw[w:w + 1, :]
    xc = cb_ref[...] + xc
    ext_sc[0:sub, :] = ext_sc[tt:tt + sub, :]

    xcb = xc.astype(BF16)
    wb = V7X_MXU_DIM
    for j in range(d // wb):
        cols = slice(j * wb, (j + 1) * wb)
        xcj = xc[:, cols]
        g = jnp.dot(xcb[:, cols], wg_ref[j], preferred_element_type=F32)
        r = jax.nn.sigmoid(g[:, :wb] + ba_ref[:, cols])
        i = jax.nn.sigmoid(g[:, wb:] + bx_ref[:, cols])
        log_a = (-RG_C * r) * _softplus(-lam_ref[:, cols])
        a = jnp.exp(log_a)
        bx = jnp.sqrt(_neg_expm1(2.0 * log_a, a)) * (i * xcj)
        a_sc[:, cols], b_sc[:, cols] = _scan_groups(a, bx)

    hc = hc_sc[...]
    for grp in range(tt // sub):
        rows = slice(grp * sub, (grp + 1) * sub)
        h = a_sc[rows, :] * hc + b_sc[rows, :]
        h_sc[rows, :] = h
        hc = h[sub - 1:sub, :]
    hc_sc[...] = hc
    hl_ref[...] = hc
    o_ref[...] = (h_sc[...] * _gelu_tanh(yr_ref[...].astype(F32))).astype(o_ref.dtype)


def _rglru(xr, yr, conv_hist, h0, cw, cb, wg, ba, bx, lam):
    b, t, d = xr.shape
    tt = min(RNN_TIME_TILE, t)
    assert t % tt == 0 and tt % V7X_SUBLANES == 0
    has_state = conv_hist is not None
    tok_spec = pl.BlockSpec((None, tt, d), lambda i, j: (i, j, 0))
    in_specs = [tok_spec, tok_spec]
    args = [xr, yr]
    if has_state:
        in_specs += [pl.BlockSpec((None, V7X_SUBLANES, d), lambda i, j: (i, 0, 0)),
                     pl.BlockSpec((None, 1, d), lambda i, j: (i, 0, 0))]
        args += [conv_hist, h0]
    weights = [cw, cb, wg, ba, bx, lam]
    in_specs += [_const_spec(w.shape) for w in weights]
    args += weights
    vmem = 4 * tt * d * 2 * 2 + 2 * tt * d * 2 + 4 * (tt + 8) * d * 4 + 16 * tt * d * 4
    return pl.pallas_call(
        functools.partial(_rglru_kernel, tt=tt, has_state=has_state),
        grid=(b, t // tt),
        in_specs=in_specs,
        out_specs=(tok_spec, pl.BlockSpec((None, 1, d), lambda i, j: (i, 0, 0))),
        out_shape=(jax.ShapeDtypeStruct((b, t, d), BF16), jax.ShapeDtypeStruct((b, 1, d), F32)),
        scratch_shapes=[pltpu.VMEM((tt + V7X_SUBLANES, d), F32), pltpu.VMEM((tt, d), F32),
                        pltpu.VMEM((tt, d), F32), pltpu.VMEM((tt, d), F32), pltpu.VMEM((1, d), F32)],
        compiler_params=_params(("arbitrary", "arbitrary"), vmem),
        name="rglru",
    )(*args)


def _out_mlp_kernel(x_ref, oa_ref, or_ref, ga_ref, gb_ref, wpa_ref, wpr_ref, wo_ref, g2_ref,
                    wup_ref, wdn_ref, gf_ref, y_ref):
    d = x_ref.shape[1]
    ya = jnp.dot(oa_ref[...], wpa_ref[...], preferred_element_type=F32)
    yb = jnp.dot(or_ref[...], wpr_ref[...], preferred_element_type=F32)
    merged = (jax.nn.sigmoid(ga_ref[...].astype(F32)) * ya
              + jax.nn.sigmoid(gb_ref[...].astype(F32)) * yb)
    x1 = x_ref[...] + jnp.dot(merged.astype(BF16), wo_ref[...], preferred_element_type=F32)
    h2 = _rmsnorm(x1, g2_ref[...]).astype(BF16)
    acc = x1
    for j in range(wup_ref.shape[1] // d):
        cols = slice(j * d, (j + 1) * d)
        u = jnp.maximum(jnp.dot(h2, wup_ref[:, cols], preferred_element_type=F32), 0.0)
        acc = acc + jnp.dot((u * u).astype(BF16), wdn_ref[cols, :], preferred_element_type=F32)
    y_ref[...] = _rmsnorm(acc, gf_ref[...])


def _out_mlp(x2d, oa, orn, ga, gb, wpa, wpr, wo, g2, wup, wdn, gf):
    n, d = x2d.shape
    tm = min(TOKEN_TILE, n)
    assert n % tm == 0
    tok_spec = pl.BlockSpec((tm, d), lambda i: (i, 0))
    weights = [wpa, wpr, wo, g2, wup, wdn, gf]
    vmem = (sum(w.size * w.dtype.itemsize for w in weights) + 2 * tm * d * (4 + 4 * 2 + 4)
            + 10 * tm * d * 4)
    return pl.pallas_call(
        _out_mlp_kernel,
        grid=(n // tm,),
        in_specs=[tok_spec] * 5 + [_const_spec(w.shape) for w in weights],
        out_specs=tok_spec,
        out_shape=jax.ShapeDtypeStruct((n, d), F32),
        compiler_params=_params(("arbitrary",), vmem),
        name="out_mlp",
    )(x2d, oa, orn, ga, gb, *weights)


def _prepare_weights(norm_mix_g, w_in, b_f, conv_w, conv_b, w_rg_a, b_rg_a, w_rg_x, b_rg_x, rg_lambda,
                     w_proj_attn, w_proj_rnn, w_out, norm_mlp_g, w_up, w_down, norm_final_g):
    d = w_in.shape[0]
    d_attn = N_HEADS * HEAD_DIM
    qkv_end = 3 * d_attn
    rest = qkv_end + N_HEADS
    split = np.array([_head_of(HEADS_SPLIT, p, s) * HEAD_DIM + f for p in range(N_PAIRS)
                      for s in range(HEADS_PER_TILE) for f in range(HEAD_DIM)])
    w_q = w_in[:, :d_attn] * (HEAD_DIM ** -0.5 * LOG2_E)
    w_qkv = jnp.stack([w_q, w_in[:, d_attn:2 * d_attn], w_in[:, 2 * d_attn:qkv_end]]).astype(BF16)
    w_gate = jnp.stack([w_in[:, rest + s * d:rest + (s + 1) * d] for s in range(4)]).astype(BF16)
    pad = V7X_LANES - N_HEADS
    wf = jnp.pad(w_in[:, qkv_end:rest], ((0, 0), (0, pad))).astype(BF16)
    bf = jnp.pad(b_f, (0, pad)).reshape(1, V7X_LANES)

    def block_diag(w):
        n_tiles = w.shape[0] // GATE_BLOCKS_PER_TILE
        w4 = w.reshape(n_tiles, GATE_BLOCKS_PER_TILE, RNN_BLOCK, RNN_BLOCK)
        eye = jnp.eye(GATE_BLOCKS_PER_TILE, dtype=w.dtype)
        return jnp.einsum('jmde,mn->jmdne', w4, eye).reshape(n_tiles, V7X_MXU_DIM, V7X_MXU_DIM)

    wg = jnp.concatenate([block_diag(w_rg_a), block_diag(w_rg_x)], axis=-1).astype(BF16)
    row = lambda v: v.reshape(1, -1)
    return dict(
        qkv_proj=(row(norm_mix_g), w_qkv), qkv_proj_split=(row(norm_mix_g), w_qkv[:, :, split]),
        gate_proj=(row(norm_mix_g), w_gate, wf, bf),
        rglru=(conv_w, row(conv_b), wg, row(b_rg_a), row(b_rg_x), row(rg_lambda)),
        proj_attn=w_proj_attn.astype(BF16), proj_attn_split=w_proj_attn[split, :].astype(BF16),
        out_mlp=(w_proj_rnn.astype(BF16), w_out.astype(BF16), row(norm_mlp_g),
                 w_up.astype(BF16), w_down.astype(BF16), row(norm_final_g)))


def _layer(x, k_hist, v_hist, logf_hist, conv_hist, h0, wts):
    b, t, d = x.shape
    x2d = x.reshape(b * t, d)
    first = k_hist is None
    q, kb, vb, k_state, v_state = _qkv_proj(x2d, *wts['qkv_proj' if first else 'qkv_proj_split'],
                                            seq_len=t, position_minor=first)
    xr, yr, ga, gb, lf, tail = _gate_proj(x2d, *wts['gate_proj'], seq_len=t)
    as_seq = lambda a: a.reshape(b, t, d)
    lf_new = lf.reshape(b, t, N_HEADS)
    if first:
        o_attn = _attention_first(as_seq(q), as_seq(kb), as_seq(vb), _cumsum(lf_new))
        new_k, new_v = (jnp.transpose(s, (0, 3, 1, 2)) for s in (k_state, v_state))
    else:
        c = _cumsum(jnp.concatenate([logf_hist, lf_new], axis=1))
        o_attn = _attention_hist(as_seq(q), as_seq(kb), as_seq(vb), k_hist, v_hist, c)
        new_k, new_v = (s.reshape(b, t, N_HEADS, HEAD_DIM) for s in (k_state, v_state))
        conv_hist = jnp.pad(conv_hist, ((0, 0), (V7X_SUBLANES - (CONV_W - 1), 0), (0, 0)))
        h0 = h0.reshape(b, 1, d)
    o_rnn, h_last = _rglru(as_seq(xr), as_seq(yr), conv_hist, h0, *wts['rglru'])
    y = _out_mlp(x2d, o_attn.reshape(b * t, d), o_rnn.reshape(b * t, d), ga, gb,
                 wts['proj_attn' if first else 'proj_attn_split'], *wts['out_mlp'])
    new_conv = tail[:, V7X_SUBLANES - (CONV_W - 1):, :]
    return y.reshape(b, t, d), new_k, new_v, lf_new, new_conv, h_last.reshape(b, d)


def kernel(x_prompt, x_sample, cache_k, cache_v, cache_logf, state_conv, state_rglru, norm_mix_g, w_in, b_f, conv_w, conv_b, w_rg_a, b_rg_a, w_rg_x, b_rg_x, rg_lambda, w_proj_attn, w_proj_rnn, w_out, norm_mlp_g, w_up, w_down, norm_final_g):
    assert w_in.shape[0] == 1, "the final norm is fused into the layer: one layer only"
    wts = _prepare_weights(norm_mix_g[0], w_in[0], b_f[0], conv_w[0], conv_b[0], w_rg_a[0], b_rg_a[0],
                           w_rg_x[0], b_rg_x[0], rg_lambda[0], w_proj_attn[0], w_proj_rnn[0], w_out[0],
                           norm_mlp_g[0], w_up[0], w_down[0], norm_final_g)
    yp, kp, vp, lp, cp, hp = _layer(x_prompt, None, None, None, None, None, wts)
    ys, ks, vs, ls, cs, hs = _layer(x_sample, cache_k[0], cache_v[0], cache_logf[0], state_conv[0],
                                    state_rglru[0], wts)
    lead = lambda a: a[None]
    return (yp, ys, lead(kp), lead(vp), lead(lp), lead(cp), lead(hp),
            lead(ks), lead(vs), lead(ls), lead(cs), lead(hs))
```

```python
import functools

import numpy as np
import jax
import jax.numpy as jnp
from jax import lax
from jax.experimental import pallas as pl
from jax.experimental.pallas import tpu as pltpu

F32 = jnp.float32
BF16 = jnp.bfloat16

N_HEADS = 16
HEAD_DIM = 64
CONV_W = 4
RNN_BLOCK = 64
RG_C = 8.0
EPS = 1e-6
LOG2_E = 1.4426950408889634

V7X_LANES = 128
V7X_SUBLANES = 8
V7X_MXU_DIM = 256
V7X_VMEM_BYTES = 64 * 1024 * 1024

HEADS_PER_TILE = V7X_LANES // HEAD_DIM
N_PAIRS = N_HEADS // HEADS_PER_TILE
GATE_BLOCKS_PER_TILE = V7X_MXU_DIM // RNN_BLOCK
TOKEN_TILE = 512
ATTN_Q_TILE = 512
RNN_TIME_TILE = 256
CUMSUM_BLOCK = 256
N_SPLIT = 3


HEADS_NATURAL = (HEADS_PER_TILE, 1)
HEADS_SPLIT = (1, N_PAIRS)


def _head_of(order, pair, slot):
    return pair * order[0] + slot * order[1]


def _vmem_limit(estimate_bytes):
    return int(min(estimate_bytes + (12 << 20), V7X_VMEM_BYTES - (6 << 20)))


def _params(semantics, vmem_estimate):
    return pltpu.CompilerParams(dimension_semantics=semantics,
                                vmem_limit_bytes=_vmem_limit(vmem_estimate))


def _const_spec(shape):
    zeros = (0,) * len(shape)
    return pl.BlockSpec(shape, lambda *_: zeros, pipeline_mode=pl.Buffered(1))


def _rmsnorm(x, g):
    x = x * lax.rsqrt(jnp.mean(x * x, axis=-1, keepdims=True) + EPS)
    return x * g


def _log_sigmoid(x):
    return jnp.minimum(x, 0.0) - jnp.log1p(jnp.exp(-jnp.abs(x)))


def _softplus(x):
    return jnp.maximum(x, 0.0) + jnp.log1p(jnp.exp(-jnp.abs(x)))


def _gelu_tanh(x):
    c = 0.7978845608028654
    return x * (0.5 * (1.0 + jnp.tanh(c * (x + 0.044715 * (x * x * x)))))


def _split3(x):
    hi = x.astype(BF16).astype(F32)
    r1 = x - hi
    mid = r1.astype(BF16).astype(F32)
    lo = (r1 - mid).astype(BF16).astype(F32)
    return hi, mid, lo


def _store_head_major(r, o_ref, slab_sc):
    tm = r.shape[0]
    sub = V7X_SUBLANES
    n_tiles = r.shape[1] // V7X_LANES
    for j in range(n_tiles):
        slab_sc[:, j * sub:(j + 1) * sub, :] = (
            r[:, j * V7X_LANES:(j + 1) * V7X_LANES].reshape(tm // sub, sub, V7X_LANES))
    for t in range(sub):
        rows = slab_sc[:, pl.ds(t, n_tiles, stride=sub), :]
        o_ref[:, t, 0:N_PAIRS, :] = rows[:, :, 0:HEAD_DIM]
        o_ref[:, t, N_PAIRS:N_HEADS, :] = rows[:, :, HEAD_DIM:V7X_LANES]


def _qkv_proj_kernel(x_ref, g_ref, w_ref, q_ref, kb_ref, vb_ref, ks_ref, vs_ref, *scratch, position_minor):
    hn = _rmsnorm(x_ref[...], g_ref[...]).astype(BF16)
    q_ref[...] = jnp.dot(hn, w_ref[0], preferred_element_type=F32).astype(q_ref.dtype)
    for s, (b_ref, state_ref) in enumerate(((kb_ref, ks_ref), (vb_ref, vs_ref)), start=1):
        r = jnp.dot(hn, w_ref[s], preferred_element_type=F32)
        b_ref[...] = r.astype(b_ref.dtype)
        if position_minor:
            state_ref[...] = r.T.reshape(state_ref.shape)
        else:
            _store_head_major(r, state_ref, *scratch)


def _qkv_proj(x2d, g, w_qkv, seq_len, position_minor):
    n, d = x2d.shape
    tm = min(TOKEN_TILE, n)
    sub = V7X_SUBLANES
    assert n % tm == 0 and tm % sub == 0 and d == N_HEADS * HEAD_DIM
    tok_spec = pl.BlockSpec((tm, d), lambda i: (i, 0))
    tok_bf16 = jax.ShapeDtypeStruct((n, d), BF16)
    if position_minor:
        assert seq_len % tm == 0
        tiles = seq_len // tm
        state_spec = pl.BlockSpec((None, N_HEADS, HEAD_DIM, tm), lambda i: (i // tiles, 0, 0, i % tiles))
        state_f32 = jax.ShapeDtypeStruct((n // seq_len, N_HEADS, HEAD_DIM, seq_len), F32)
        scratch, state_bytes = [], tm * d * 4
    else:
        state_spec = pl.BlockSpec((tm // sub, sub, N_HEADS, HEAD_DIM), lambda i: (i, 0, 0, 0))
        state_f32 = jax.ShapeDtypeStruct((n // sub, sub, N_HEADS, HEAD_DIM), F32)
        scratch = [pltpu.VMEM((tm // sub, d // V7X_LANES * sub, V7X_LANES), F32)]
        state_bytes = tm * N_HEADS * V7X_LANES * 4
    vmem = w_qkv.size * 2 + 2 * tm * d * 4 + 2 * 3 * tm * d * 2 + 2 * 2 * state_bytes + 6 * tm * d * 4
    return pl.pallas_call(
        functools.partial(_qkv_proj_kernel, position_minor=position_minor),
        grid=(n // tm,),
        in_specs=[tok_spec, _const_spec(g.shape), _const_spec(w_qkv.shape)],
        out_specs=(tok_spec, tok_spec, tok_spec, state_spec, state_spec),
        out_shape=(tok_bf16, tok_bf16, tok_bf16, state_f32, state_f32),
        scratch_shapes=scratch,
        compiler_params=_params(("arbitrary",), vmem),
        name="qkv_proj",
    )(x2d, g, w_qkv)


def _gate_proj_kernel(x_ref, g_ref, w_ref, wf_ref, bf_ref,
                      xr_ref, yr_ref, ga_ref, gb_ref, lf_ref, tail_ref, *, seg_len):
    hn = _rmsnorm(x_ref[...], g_ref[...]).astype(BF16)
    tm = hn.shape[0]
    for s, o_ref in enumerate((xr_ref, yr_ref, ga_ref, gb_ref)):
        r = jnp.dot(hn, w_ref[s], preferred_element_type=F32)
        o_ref[...] = r.astype(o_ref.dtype)
        if o_ref is xr_ref:
            for j in range(tm // seg_len):
                end = (j + 1) * seg_len
                tail_ref[j] = r[end - V7X_SUBLANES:end, :]
    zf = jnp.dot(hn, wf_ref[...], preferred_element_type=F32)
    lf_ref[...] = _log_sigmoid(zf + bf_ref[...])[:, :N_HEADS]


def _gate_proj(x2d, g, w_gate, wf, bf, seq_len):
    n, d = x2d.shape
    tm = min(TOKEN_TILE, n)
    assert n % tm == 0 and (seq_len % tm == 0 or tm % seq_len == 0)
    seg_len = min(seq_len, tm)
    nseg = tm // seg_len
    tiles_per_stream = max(1, seq_len // tm)
    row = lambda i: (i, 0)
    tok_bf16 = jax.ShapeDtypeStruct((n, d), BF16)
    tok_spec = pl.BlockSpec((tm, d), row)
    out_shape = (tok_bf16,) * 4 + (jax.ShapeDtypeStruct((n, N_HEADS), F32),
                                   jax.ShapeDtypeStruct((n // seq_len, V7X_SUBLANES, d), F32))
    out_specs = (tok_spec,) * 4 + (
        pl.BlockSpec((tm, N_HEADS), row),
        pl.BlockSpec((nseg, V7X_SUBLANES, d), lambda i: (i // tiles_per_stream, 0, 0)))
    vmem = (w_gate.size + wf.size) * 2 + 2 * tm * d * 4 + 2 * 4 * tm * d * 2 + 5 * tm * d * 4
    return pl.pallas_call(
        functools.partial(_gate_proj_kernel, seg_len=seg_len),
        grid=(n // tm,),
        in_specs=[tok_spec, _const_spec(g.shape), _const_spec(w_gate.shape),
                  _const_spec(wf.shape), _const_spec(bf.shape)],
        out_specs=out_specs,
        out_shape=out_shape,
        compiler_params=_params(("arbitrary",), vmem),
        name="gate_proj",
    )(x2d, g, w_gate, wf, bf)


def _cumsum_kernel(lf_ref, c_ref):
    tk = lf_ref.shape[0]
    blk = min(CUMSUM_BLOCK, tk)
    row = lax.broadcasted_iota(jnp.int32, (blk, blk), 0)
    col = lax.broadcasted_iota(jnp.int32, (blk, blk), 1)
    lower = (row >= col).astype(BF16)
    carry = jnp.zeros((1, N_HEADS), F32)
    for s in range(0, tk, blk):
        sz = min(blk, tk - s)
        c = carry
        for part in _split3(lf_ref[s:s + sz, :]):
            c = c + jnp.dot(lower[:sz, :sz], part.astype(BF16), preferred_element_type=F32)
        c_ref[s:s + sz, :] = c * LOG2_E
        carry = c[sz - 1:sz, :]


def _cumsum(lf):
    b, tk, h = lf.shape
    spec = pl.BlockSpec((None, tk, h), lambda i: (i, 0, 0))
    return pl.pallas_call(
        _cumsum_kernel,
        grid=(b,),
        in_specs=[spec],
        out_specs=spec,
        out_shape=jax.ShapeDtypeStruct((b, tk, h), F32),
        compiler_params=_params(("arbitrary",), 8 * tk * V7X_LANES * 4),
        name="logf_cumsum",
    )(lf)


def _augment(own, lane, base, data, ones_first, parts):
    aug = jnp.zeros(lane.shape, F32)
    for i, part in enumerate(parts):
        at = base + i + (N_SPLIT if ones_first else 0)
        aug = jnp.where(lane == at, part, aug)
    ones_at = base if ones_first else base + N_SPLIT
    aug = jnp.where((lane >= ones_at) & (lane < ones_at + N_SPLIT), 1.0, aug)
    return aug if data is None else jnp.where(own, data.astype(F32), aug)


def _attn_first_kernel(q_ref, k_ref, v_ref, c_ref, o_ref, kt_sc, v_sc, *, tq):
    pair = pl.program_id(1)
    qi = pl.program_id(2)
    n_chunks = k_ref.shape[0] // tq
    lane = lax.broadcasted_iota(jnp.int32, (tq, V7X_LANES), 1)
    own = (lane < HEAD_DIM, lane >= HEAD_DIM)
    base = (HEAD_DIM, 0)

    def head_columns(c_rows):
        head = lax.broadcasted_iota(jnp.int32, c_rows.shape, 1)
        return [jnp.sum(jnp.where(head == _head_of(HEADS_NATURAL, pair, h), c_rows, 0.0), axis=1, keepdims=True)
                for h in range(HEADS_PER_TILE)]

    def stage_keys():
        for j in range(n_chunks):
            rows = slice(j * tq, (j + 1) * tq)
            k = k_ref[rows, :]
            v = v_ref[rows, :]
            c_cols = head_columns(c_ref[rows, :])
            for h in range(HEADS_PER_TILE):
                neg_parts = [-part for part in _split3(c_cols[h])]
                k_wide = _augment(own[h], lane, base[h], k, False, neg_parts)
                kt_sc[h, j] = k_wide.T.astype(BF16)
                v_sc[h, rows, :] = jnp.where(own[h], v.astype(F32),
                                             jnp.where(lane == base[h], 1.0, 0.0)).astype(BF16)

    causal = (lax.broadcasted_iota(jnp.int32, (tq, tq), 1)
              <= lax.broadcasted_iota(jnp.int32, (tq, tq), 0))

    def query_tile(n):
        if n == 0:
            stage_keys()
        q = q_ref[...]
        cq_cols = head_columns(c_ref[n * tq:(n + 1) * tq, :])
        for h in range(HEADS_PER_TILE):
            q_wide = _augment(own[h], lane, base[h], q, True, _split3(cq_cols[h])).astype(BF16)
            s = jnp.dot(q_wide, kt_sc[h, n], preferred_element_type=F32)
            s = jnp.where(causal, s, -jnp.inf)
            m = jnp.max(s, axis=1, keepdims=True)
            p = jnp.exp2(s - m).astype(BF16)
            acc = jnp.dot(p, v_sc[h, n * tq:(n + 1) * tq, :], preferred_element_type=F32)
            for j in range(n):
                s = jnp.dot(q_wide, kt_sc[h, j], preferred_element_type=F32)
                m_new = jnp.maximum(m, jnp.max(s, axis=1, keepdims=True))
                p = jnp.exp2(s - m_new).astype(BF16)
                acc = (jnp.exp2(m - m_new) * acc
                       + jnp.dot(p, v_sc[h, j * tq:(j + 1) * tq, :], preferred_element_type=F32))
                m = m_new
            out = acc / acc[:, base[h]:base[h] + 1]
            lanes_h = slice(h * HEAD_DIM, (h + 1) * HEAD_DIM)
            o_ref[:, lanes_h] = out[:, lanes_h].astype(o_ref.dtype)

    for n in range(n_chunks):
        pl.when(qi == n)(functools.partial(query_tile, n))


def _attention_first(q, k, v, c):
    b, t, d = q.shape
    tq = min(ATTN_Q_TILE, t)
    assert t % tq == 0
    lanes = V7X_LANES
    seq_spec = pl.BlockSpec((None, t, lanes), lambda i, p, j: (i, 0, p))
    q_spec = pl.BlockSpec((None, tq, lanes), lambda i, p, j: (i, j, p))
    vmem = (4 * t * lanes * 2 + 2 * t * lanes * 4 + 4 * t * lanes * 2 + 10 * tq * tq * 4
            + 8 * tq * lanes * 4)
    return pl.pallas_call(
        functools.partial(_attn_first_kernel, tq=tq),
        grid=(b, N_PAIRS, t // tq),
        in_specs=[q_spec, seq_spec, seq_spec, pl.BlockSpec((None, t, N_HEADS), lambda i, p, j: (i, 0, 0))],
        out_specs=q_spec,
        out_shape=jax.ShapeDtypeStruct((b, t, d), BF16),
        scratch_shapes=[pltpu.VMEM((HEADS_PER_TILE, t // tq, lanes, tq), BF16),
                        pltpu.VMEM((HEADS_PER_TILE, t, lanes), BF16)],
        compiler_params=_params(("arbitrary", "arbitrary", "arbitrary"), vmem),
        name="fox_attention_first",
    )(q, k, v, c)


def _attn_hist_kernel(q_ref, kn_ref, vn_ref, kh_ref, vh_ref, c_ref, o_ref, k_sc, v_sc, *, past):
    t = q_ref.shape[0]
    tk = past + t
    lanes = V7X_LANES
    pad_rows = k_sc.shape[0] - tk
    lane = lax.broadcasted_iota(jnp.int32, (tk, lanes), 1)
    sq_row = lax.broadcasted_iota(jnp.int32, (lanes, lanes), 0)
    sq_col = lax.broadcasted_iota(jnp.int32, (lanes, lanes), 1)
    causal = (lax.broadcasted_iota(jnp.int32, (t, lanes), 1)
              <= lax.broadcasted_iota(jnp.int32, (t, lanes), 0))
    key_parts = _split3(c_ref[...])
    query_parts = _split3(c_ref[past:tk, :])
    lane_q = lax.broadcasted_iota(jnp.int32, (t, lanes), 1)
    ones_col =jnp.where(lane == HEAD_DIM, 1.0, 0.0).astype(BF16)
    zero_pad = jnp.zeros((pad_rows, lanes), BF16)
    k_sc[tk:tk + pad_rows, :] = zero_pad
    v_sc[tk:tk + pad_rows, :] = zero_pad

    for pair in range(N_PAIRS):
        cols = slice(pair * lanes, (pair + 1) * lanes)
        q_pair, kn_pair, vn_pair = q_ref[:, cols], kn_ref[:, cols], vn_ref[:, cols]
        o_pair = jnp.zeros((t, lanes), F32)
        for slot in range(HEADS_PER_TILE):
            head = _head_of(HEADS_SPLIT, pair, slot)
            shift = slot * HEAD_DIM
            to_low = ((sq_row == sq_col + shift) & (sq_col < HEAD_DIM)).astype(BF16)
            to_slot = ((sq_col == sq_row + shift) & (sq_row < HEAD_DIM)).astype(BF16)
            low = lambda a: jnp.dot(a, to_low, preferred_element_type=F32)
            neg_parts = [-part[:, head:head + 1] for part in key_parts]
            k_sc[0:tk, :] = _augment(None, lane, HEAD_DIM, None, False, neg_parts).astype(BF16)
            v_sc[0:tk, :] = ones_col
            hist_cols = slice(head // HEADS_PER_TILE * lanes, (head // HEADS_PER_TILE + 1) * lanes)
            for hist_ref, sc in ((kh_ref, k_sc), (vh_ref, v_sc)):
                tile = hist_ref[:, hist_cols]
                if head % HEADS_PER_TILE:
                    tile = pltpu.roll(tile, HEAD_DIM, 1)
                sc[0:past, 0:HEAD_DIM] = tile[:, 0:HEAD_DIM].astype(BF16)
            k_sc[past:tk, 0:HEAD_DIM] = low(kn_pair)[:, 0:HEAD_DIM].astype(BF16)
            v_sc[past:tk, 0:HEAD_DIM] = low(vn_pair)[:, 0:HEAD_DIM].astype(BF16)
            q_wide = _augment(lane_q < HEAD_DIM, lane_q, HEAD_DIM, low(q_pair), True,
                              [part[:, head:head + 1] for part in query_parts]).astype(BF16)
            nt = (((1,), (1,)), ((), ()))
            s_hist = lax.dot_general(q_wide, k_sc[0:past, :], nt, preferred_element_type=F32)
            s_new = lax.dot_general(q_wide, k_sc[past:past + lanes, :], nt, preferred_element_type=F32)
            s_new = jnp.where(causal, s_new, -jnp.inf)
            m = jnp.maximum(jnp.max(s_hist, axis=1, keepdims=True), jnp.max(s_new, axis=1, keepdims=True))
            acc = (jnp.dot(jnp.exp2(s_hist - m).astype(BF16), v_sc[0:past, :], preferred_element_type=F32)
                   + jnp.dot(jnp.exp2(s_new - m).astype(BF16), v_sc[past:past + lanes, :],
                             preferred_element_type=F32))
            o_head = (acc / acc[:, HEAD_DIM:HEAD_DIM + 1]).astype(BF16)
            o_pair = o_pair + jnp.dot(o_head, to_slot, preferred_element_type=F32)
        o_ref[:, cols] = o_pair.astype(o_ref.dtype)


def _attention_hist(q, k_new, v_new, k_hist, v_hist, c):
    b, t, d = q.shape
    past = k_hist.shape[1]
    lanes = V7X_LANES
    assert t <= lanes and t % V7X_SUBLANES == 0 and past % lanes == 0
    k_rows = k_hist.reshape(b, past, d)
    v_rows = v_hist.reshape(b, past, d)
    new_spec = pl.BlockSpec((None, t, d), lambda i: (i, 0, 0))
    hist_spec = pl.BlockSpec((None, past, d), lambda i: (i, 0, 0))
    vmem = (2 * 2 * past * N_HEADS * lanes * 4 + 8 * t * d * 2 + 2 * (past + t) * lanes * 4
            + 2 * (past + lanes) * lanes * 2 + 16 * (past + t) * lanes * 4)
    return pl.pallas_call(
        functools.partial(_attn_hist_kernel, past=past),
        grid=(b,),
        in_specs=[new_spec, new_spec, new_spec, hist_spec, hist_spec,
                  pl.BlockSpec((None, past + t, N_HEADS), lambda i: (i, 0, 0))],
        out_specs=new_spec,
        out_shape=jax.ShapeDtypeStruct((b, t, d), BF16),
        scratch_shapes=[pltpu.VMEM((past + lanes, lanes), BF16), pltpu.VMEM((past + lanes, lanes), BF16)],
        compiler_params=_params(("arbitrary",), vmem),
        name="fox_attention_hist",
    )(q, k_new, v_new, k_rows, v_rows, c)


def _neg_expm1(y, exp_half_y):
    series = y * (-1.0 / 120.0) - 1.0 / 24.0
    for coeff in (-1.0 / 6.0, -0.5, -1.0):
        series = series * y + coeff
    return jnp.where(y > -0.0625, y * series, 1.0 - exp_half_y * exp_half_y)


def _scan_groups(a, b):
    rows, width = a.shape
    sub = V7X_SUBLANES
    a = a.reshape(rows // sub, sub, width)
    b = b.reshape(rows // sub, sub, width)
    row = lax.broadcasted_iota(jnp.int32, a.shape, 1)
    for shift in (1, 2, 4):
        keep = row >= shift
        a_prev = jnp.where(keep, pltpu.roll(a, shift, 1), 1.0)
        b_prev = jnp.where(keep, pltpu.roll(b, shift, 1), 0.0)
        b = a * b_prev + b
        a = a * a_prev
    return a.reshape(rows, width), b.reshape(rows, width)


def _rnn_tile(xr, yr, weight_refs, state_scratch):
    cw_ref, cb_ref, wg_ref, ba_ref, bx_ref, lam_ref = weight_refs
    ext_sc, a_sc, b_sc, h_sc, hc_sc = state_scratch
    sub = V7X_SUBLANES
    tt, d = xr.shape
    ext_sc[sub:sub + tt, :] = xr
    cw = cw_ref[...]
    xc = ext_sc[sub - 3:sub - 3 + tt, :] * cw[0:1, :]
    for w in range(1, CONV_W):
        xc = xc + ext_sc[sub - 3 + w:sub - 3 + w + tt, :] * cw[w:w + 1, :]
    xc = cb_ref[...] + xc
    ext_sc[0:sub, :] = ext_sc[tt:tt + sub, :]

    xcb = xc.astype(BF16)
    wb = V7X_MXU_DIM
    for j in range(d // wb):
        cols = slice(j * wb, (j + 1) * wb)
        xcj = xc[:, cols]
        g = jnp.dot(xcb[:, cols], wg_ref[j], preferred_element_type=F32)
        r = jax.nn.sigmoid(g[:, :wb] + ba_ref[:, cols])
        i = jax.nn.sigmoid(g[:, wb:] + bx_ref[:, cols])
        log_a = (-RG_C * r) * _softplus(-lam_ref[:, cols])
        a = jnp.exp(log_a)
        bx = jnp.sqrt(_neg_expm1(2.0 * log_a, a)) * (i * xcj)
        a_sc[:, cols], b_sc[:, cols] = _scan_groups(a, bx)

    hc = hc_sc[...]
    for grp in range(tt // sub):
        rows = slice(grp * sub, (grp + 1) * sub)
        h = a_sc[rows, :] * hc + b_sc[rows, :]
        h_sc[rows, :] = h
        hc = h[sub - 1:sub, :]
    hc_sc[...] = hc
    return h_sc[...] * _gelu_tanh(yr)


def _init_rnn_state(state_scratch, conv_rows, h0):
    ext_sc, _, _, _, hc_sc = state_scratch
    ext_sc[0:V7X_SUBLANES, :] = conv_rows
    hc_sc[...] = h0


def _rnn_scratch(tt, d):
    return [pltpu.VMEM((tt + V7X_SUBLANES, d), F32), pltpu.VMEM((tt, d), F32),
            pltpu.VMEM((tt, d), F32), pltpu.VMEM((tt, d), F32), pltpu.VMEM((1, d), F32)]


def _rglru_kernel(*refs, has_state):
    if has_state:
        xr_ref, yr_ref, ch_ref, h0_ref = refs[:4]
        refs = refs[4:]
    else:
        xr_ref, yr_ref = refs[:2]
        refs = refs[2:]
    weight_refs, (o_ref, hl_ref), state_scratch = refs[:6], refs[6:8], refs[8:]
    d = xr_ref.shape[1]

    @pl.when(pl.program_id(1) == 0)
    def _first_tile():
        if has_state:
            _init_rnn_state(state_scratch, ch_ref[...], h0_ref[...])
        else:
            _init_rnn_state(state_scratch, jnp.zeros((V7X_SUBLANES, d), F32), jnp.zeros((1, d), F32))

    out = _rnn_tile(xr_ref[...].astype(F32), yr_ref[...].astype(F32), weight_refs, state_scratch)
    o_ref[...] = out.astype(o_ref.dtype)
    hl_ref[...] = state_scratch[4][...]


def _rglru(xr, yr, conv_hist, h0, cw, cb, wg, ba, bx, lam):
    b, t, d = xr.shape
    tt = min(RNN_TIME_TILE, t)
    assert t % tt == 0 and tt % V7X_SUBLANES == 0
    has_state = conv_hist is not None
    tok_spec = pl.BlockSpec((None, tt, d), lambda i, j: (i, j, 0))
    in_specs = [tok_spec, tok_spec]
    args = [xr, yr]
    if has_state:
        in_specs += [pl.BlockSpec((None, V7X_SUBLANES, d), lambda i, j: (i, 0, 0)),
                     pl.BlockSpec((None, 1, d), lambda i, j: (i, 0, 0))]
        args += [conv_hist, h0]
    weights = [cw, cb, wg, ba, bx, lam]
    in_specs += [_const_spec(w.shape) for w in weights]
    args += weights
    vmem = 4 * tt * d * 2 * 2 + 2 * tt * d * 2 + 4 * (tt + 8) * d * 4 + 16 * tt * d * 4
    return pl.pallas_call(
        functools.partial(_rglru_kernel, has_state=has_state),
        grid=(b, t // tt),
        in_specs=in_specs,
        out_specs=(tok_spec, pl.BlockSpec((None, 1, d), lambda i, j: (i, 0, 0))),
        out_shape=(jax.ShapeDtypeStruct((b, t, d), BF16), jax.ShapeDtypeStruct((b, 1, d), F32)),
        scratch_shapes=_rnn_scratch(tt, d),
        compiler_params=_params(("arbitrary", "arbitrary"), vmem),
        name="rglru",
    )(*args)


def _gate_rnn_kernel(x_ref, g_ref, w_ref, wf_ref, bf_ref, cw_ref, cb_ref, wg_ref, ba_ref, bx_ref, lam_ref,
                     o_ref, ga_ref, gb_ref, lf_ref, tail_ref, hl_ref, *state_scratch, tiles_per_stream):
    tm, d = x_ref.shape

    @pl.when(pl.program_id(0) % tiles_per_stream == 0)
    def _first_tile():
        _init_rnn_state(state_scratch, jnp.zeros((V7X_SUBLANES, d), F32), jnp.zeros((1, d), F32))

    hn = _rmsnorm(x_ref[...], g_ref[...]).astype(BF16)
    xr = jnp.dot(hn, w_ref[0], preferred_element_type=F32)
    tail_ref[0] = xr[tm - V7X_SUBLANES:tm, :]
    yr = jnp.dot(hn, w_ref[1], preferred_element_type=F32)
    ga_ref[...] = jnp.dot(hn, w_ref[2], preferred_element_type=F32).astype(ga_ref.dtype)
    gb_ref[...] = jnp.dot(hn, w_ref[3], preferred_element_type=F32).astype(gb_ref.dtype)
    zf = jnp.dot(hn, wf_ref[...], preferred_element_type=F32)
    lf_ref[...] = _log_sigmoid(zf + bf_ref[...])[:, :N_HEADS]
    weight_refs = (cw_ref, cb_ref, wg_ref, ba_ref, bx_ref, lam_ref)
    o_ref[...] = _rnn_tile(xr, yr, weight_refs, state_scratch).astype(o_ref.dtype)
    hl_ref[...] = state_scratch[4][...]


def _gate_rnn_proj(x2d, g, w_gate, wf, bf, rnn_weights, seq_len):
    n, d = x2d.shape
    tm = min(TOKEN_TILE, n)
    assert n % tm == 0 and seq_len % tm == 0
    tiles_per_stream = seq_len // tm
    n_streams = n // seq_len
    row = lambda i: (i, 0)
    stream = lambda i: (i // tiles_per_stream, 0, 0)
    tok_bf16 = jax.ShapeDtypeStruct((n, d), BF16)
    tok_spec = pl.BlockSpec((tm, d), row)
    weights = [g, w_gate, wf, bf, *rnn_weights]
    vmem = (sum(w.size * w.dtype.itemsize for w in weights) + 2 * tm * d * 4 + 2 * 3 * tm * d * 2
            + 4 * (tm + 8) * d * 4 + 14 * tm * d * 4)
    return pl.pallas_call(
        functools.partial(_gate_rnn_kernel, tiles_per_stream=tiles_per_stream),
        grid=(n // tm,),
        in_specs=[tok_spec] + [_const_spec(w.shape) for w in weights],
        out_specs=(tok_spec, tok_spec, tok_spec, pl.BlockSpec((tm, N_HEADS), row),
                   pl.BlockSpec((1, V7X_SUBLANES, d), stream), pl.BlockSpec((None, 1, d), stream)),
        out_shape=(tok_bf16, tok_bf16, tok_bf16, jax.ShapeDtypeStruct((n, N_HEADS), F32),
                   jax.ShapeDtypeStruct((n_streams, V7X_SUBLANES, d), F32),
                   jax.ShapeDtypeStruct((n_streams, 1, d), F32)),
        scratch_shapes=_rnn_scratch(tm, d),
        compiler_params=_params(("arbitrary",), vmem),
        name="gate_rnn_proj",
    )(x2d, *weights)


def _out_mlp_kernel(x_ref, oa_ref, or_ref, ga_ref, gb_ref, wpa_ref, wpr_ref, wo_ref, g2_ref,
                    wup_ref, wdn_ref, gf_ref, y_ref):
    d = x_ref.shape[1]
    ya = jnp.dot(oa_ref[...], wpa_ref[...], preferred_element_type=F32)
    yb = jnp.dot(or_ref[...], wpr_ref[...], preferred_element_type=F32)
    merged = (jax.nn.sigmoid(ga_ref[...].astype(F32)) * ya
              + jax.nn.sigmoid(gb_ref[...].astype(F32)) * yb)
    x1 = x_ref[...] + jnp.dot(merged.astype(BF16), wo_ref[...], preferred_element_type=F32)
    h2 = _rmsnorm(x1, g2_ref[...]).astype(BF16)
    acc = x1
    for j in range(wup_ref.shape[1] // d):
        cols = slice(j * d, (j + 1) * d)
        u = jnp.maximum(jnp.dot(h2, wup_ref[:, cols], preferred_element_type=F32), 0.0)
        acc = acc + jnp.dot((u * u).astype(BF16), wdn_ref[cols, :], preferred_element_type=F32)
    y_ref[...] = _rmsnorm(acc, gf_ref[...])


def _out_mlp(x2d, oa, orn, ga, gb, wpa, wpr, wo, g2, wup, wdn, gf):
    n, d = x2d.shape
    tm = min(TOKEN_TILE, n)
    assert n % tm == 0
    tok_spec = pl.BlockSpec((tm, d), lambda i: (i, 0))
    weights = [wpa, wpr, wo, g2, wup, wdn, gf]
    vmem = (sum(w.size * w.dtype.itemsize for w in weights) + 2 * tm * d * (4 + 4 * 2 + 4)
            + 10 * tm * d * 4)
    return pl.pallas_call(
        _out_mlp_kernel,
        grid=(n // tm,),
        in_specs=[tok_spec] * 5 + [_const_spec(w.shape) for w in weights],
        out_specs=tok_spec,
        out_shape=jax.ShapeDtypeStruct((n, d), F32),
        compiler_params=_params(("arbitrary",), vmem),
        name="out_mlp",
    )(x2d, oa, orn, ga, gb, *weights)


def _prepare_weights(norm_mix_g, w_in, b_f, conv_w, conv_b, w_rg_a, b_rg_a, w_rg_x, b_rg_x, rg_lambda,
                     w_proj_attn, w_proj_rnn, w_out, norm_mlp_g, w_up, w_down, norm_final_g):
    d = w_in.shape[0]
    d_attn = N_HEADS * HEAD_DIM
    qkv_end = 3 * d_attn
    rest = qkv_end + N_HEADS
    split = np.array([_head_of(HEADS_SPLIT, p, s) * HEAD_DIM + f for p in range(N_PAIRS)
                      for s in range(HEADS_PER_TILE) for f in range(HEAD_DIM)])
    w_q = w_in[:, :d_attn] * (HEAD_DIM ** -0.5 * LOG2_E)
    w_qkv = jnp.stack([w_q, w_in[:, d_attn:2 * d_attn], w_in[:, 2 * d_attn:qkv_end]]).astype(BF16)
    w_gate = jnp.stack([w_in[:, rest + s * d:rest + (s + 1) * d] for s in range(4)]).astype(BF16)
    pad = V7X_LANES - N_HEADS
    wf = jnp.pad(w_in[:, qkv_end:rest], ((0, 0), (0, pad))).astype(BF16)
    bf = jnp.pad(b_f, (0, pad)).reshape(1, V7X_LANES)

    def block_diag(w):
        n_tiles = w.shape[0] // GATE_BLOCKS_PER_TILE
        w4 = w.reshape(n_tiles, GATE_BLOCKS_PER_TILE, RNN_BLOCK, RNN_BLOCK)
        eye = jnp.eye(GATE_BLOCKS_PER_TILE, dtype=w.dtype)
        return jnp.einsum('jmde,mn->jmdne', w4, eye).reshape(n_tiles, V7X_MXU_DIM, V7X_MXU_DIM)

    wg = jnp.concatenate([block_diag(w_rg_a), block_diag(w_rg_x)], axis=-1).astype(BF16)
    row = lambda v: v.reshape(1, -1)
    return dict(
        qkv_proj=(row(norm_mix_g), w_qkv), qkv_proj_split=(row(norm_mix_g), w_qkv[:, :, split]),
        gate_proj=(row(norm_mix_g), w_gate, wf, bf),
        rglru=(conv_w, row(conv_b), wg, row(b_rg_a), row(b_rg_x), row(rg_lambda)),
        proj_attn=w_proj_attn.astype(BF16), proj_attn_split=w_proj_attn[split, :].astype(BF16),
        out_mlp=(w_proj_rnn.astype(BF16), w_out.astype(BF16), row(norm_mlp_g),
                 w_up.astype(BF16), w_down.astype(BF16), row(norm_final_g)))


def _layer(x, k_hist, v_hist, logf_hist, conv_hist, h0, wts):
    b, t, d = x.shape
    x2d = x.reshape(b * t, d)
    first = k_hist is None
    q, kb, vb, k_state, v_state = _qkv_proj(x2d, *wts['qkv_proj' if first else 'qkv_proj_split'],
                                            seq_len=t, position_minor=first)
    as_seq = lambda a: a.reshape(b, t, d)
    if first:
        o_rnn, ga, gb, lf, tail, h_last = _gate_rnn_proj(x2d, *wts['gate_proj'], wts['rglru'], seq_len=t)
        lf_new = lf.reshape(b, t, N_HEADS)
        o_attn = _attention_first(as_seq(q), as_seq(kb), as_seq(vb), _cumsum(lf_new))
        new_k, new_v = (jnp.transpose(s, (0, 3, 1, 2)) for s in (k_state, v_state))
    else:
        xr, yr, ga, gb, lf, tail = _gate_proj(x2d, *wts['gate_proj'], seq_len=t)
        lf_new = lf.reshape(b, t, N_HEADS)
        c = _cumsum(jnp.concatenate([logf_hist, lf_new], axis=1))
        o_attn = _attention_hist(as_seq(q), as_seq(kb), as_seq(vb), k_hist, v_hist, c)
        new_k, new_v = (s.reshape(b, t, N_HEADS, HEAD_DIM) for s in (k_state, v_state))
        conv_hist = jnp.pad(conv_hist, ((0, 0), (V7X_SUBLANES - (CONV_W - 1), 0), (0, 0)))
        o_rnn, h_last = _rglru(as_seq(xr), as_seq(yr), conv_hist, h0.reshape(b, 1, d), *wts['rglru'])
    y = _out_mlp(x2d, o_attn.reshape(b * t, d), o_rnn.reshape(b * t, d), ga, gb,
                 wts['proj_attn' if first else 'proj_attn_split'], *wts['out_mlp'])
    new_conv = tail[:, V7X_SUBLANES - (CONV_W - 1):, :]
    return y.reshape(b, t, d), new_k, new_v, lf_new, new_conv, h_last.reshape(b, d)


def kernel(x_prompt, x_sample, cache_k, cache_v, cache_logf, state_conv, state_rglru, norm_mix_g, w_in, b_f, conv_w, conv_b, w_rg_a, b_rg_a, w_rg_x, b_rg_x, rg_lambda, w_proj_attn, w_proj_rnn, w_out, norm_mlp_g, w_up, w_down, norm_final_g):
    assert w_in.shape[0] == 1, "the final norm is fused into the layer: one layer only"
    wts = _prepare_weights(norm_mix_g[0], w_in[0], b_f[0], conv_w[0], conv_b[0], w_rg_a[0], b_rg_a[0],
                           w_rg_x[0], b_rg_x[0], rg_lambda[0], w_proj_attn[0], w_proj_rnn[0], w_out[0],
                           norm_mlp_g[0], w_up[0], w_down[0], norm_final_g)
    yp, kp, vp, lp, cp, hp = _layer(x_prompt, None, None, None, None, None, wts)
    ys, ks, vs, ls, cs, hs = _layer(x_sample, cache_k[0], cache_v[0], cache_logf[0], state_conv[0],
                                    state_rglru[0], wts)
    lead = lambda a: a[None]
    return (yp, ys, lead(kp), lead(vp), lead(lp), lead(cp), lead(hp),
            lead(ks), lead(vs), lead(ls), lead(cs), lead(hs))
```

```python
import functools

import numpy as np
import jax
import jax.numpy as jnp
from jax import lax
from jax.experimental import pallas as pl
from jax.experimental.pallas import tpu as pltpu

F32 = jnp.float32
BF16 = jnp.bfloat16

N_HEADS = 16
HEAD_DIM = 64
CONV_W = 4
RNN_BLOCK = 64
RG_C = 8.0
EPS = 1e-6
LOG2_E = 1.4426950408889634

V7X_LANES = 128
V7X_SUBLANES = 8
V7X_MXU_DIM = 256
V7X_VMEM_BYTES = 64 * 1024 * 1024

HEADS_PER_TILE = V7X_LANES // HEAD_DIM
N_PAIRS = N_HEADS // HEADS_PER_TILE
GATE_BLOCKS_PER_TILE = V7X_MXU_DIM // RNN_BLOCK
TOKEN_TILE = 512
ATTN_Q_TILE = 512
RNN_TIME_TILE = 256
CUMSUM_BLOCK = 256
N_SPLIT = 3


HEADS_NATURAL = (HEADS_PER_TILE, 1)
HEADS_SPLIT = (1, N_PAIRS)


def _head_of(order, pair, slot):
    return pair * order[0] + slot * order[1]


def _vmem_limit(estimate_bytes):
    return int(min(estimate_bytes + (12 << 20), V7X_VMEM_BYTES - (6 << 20)))


def _params(semantics, vmem_estimate):
    return pltpu.CompilerParams(dimension_semantics=semantics,
                                vmem_limit_bytes=_vmem_limit(vmem_estimate))


def _const_spec(shape):
    zeros = (0,) * len(shape)
    return pl.BlockSpec(shape, lambda *_: zeros, pipeline_mode=pl.Buffered(1))


def _rmsnorm(x, g):
    x = x * lax.rsqrt(jnp.mean(x * x, axis=-1, keepdims=True) + EPS)
    return x * g


def _log_sigmoid(x):
    return jnp.minimum(x, 0.0) - jnp.log1p(jnp.exp(-jnp.abs(x)))


def _softplus(x):
    return jnp.maximum(x, 0.0) + jnp.log1p(jnp.exp(-jnp.abs(x)))


def _gelu_tanh(x):
    c = 0.7978845608028654
    return x * (0.5 * (1.0 + jnp.tanh(c * (x + 0.044715 * (x * x * x)))))


def _split3(x):
    hi = x.astype(BF16).astype(F32)
    r1 = x - hi
    mid = r1.astype(BF16).astype(F32)
    lo = (r1 - mid).astype(BF16).astype(F32)
    return hi, mid, lo


def _store_head_major(r, o_ref, slab_sc):
    tm = r.shape[0]
    sub = V7X_SUBLANES
    n_tiles = r.shape[1] // V7X_LANES
    for j in range(n_tiles):
        slab_sc[:, j * sub:(j + 1) * sub, :] = (
            r[:, j * V7X_LANES:(j + 1) * V7X_LANES].reshape(tm // sub, sub, V7X_LANES))
    for t in range(sub):
        rows = slab_sc[:, pl.ds(t, n_tiles, stride=sub), :]
        o_ref[:, t, 0:N_PAIRS, :] = rows[:, :, 0:HEAD_DIM]
        o_ref[:, t, N_PAIRS:N_HEADS, :] = rows[:, :, HEAD_DIM:V7X_LANES]


def _qkv_proj_kernel(x_ref, g_ref, w_ref, q_ref, kb_ref, vb_ref, ks_ref, vs_ref, *scratch, position_minor):
    hn = _rmsnorm(x_ref[...], g_ref[...]).astype(BF16)
    q_ref[...] = jnp.dot(hn, w_ref[0], preferred_element_type=F32).astype(q_ref.dtype)
    for s, (b_ref, state_ref) in enumerate(((kb_ref, ks_ref), (vb_ref, vs_ref)), start=1):
        r = jnp.dot(hn, w_ref[s], preferred_element_type=F32)
        b_ref[...] = r.astype(b_ref.dtype)
        if position_minor:
            state_ref[...] = r.T.reshape(state_ref.shape)
        else:
            _store_head_major(r, state_ref, *scratch)


def _qkv_proj(x2d, g, w_qkv, seq_len, position_minor):
    n, d = x2d.shape
    tm = min(TOKEN_TILE, n)
    sub = V7X_SUBLANES
    assert n % tm == 0 and tm % sub == 0 and d == N_HEADS * HEAD_DIM
    tok_spec = pl.BlockSpec((tm, d), lambda i: (i, 0))
    tok_bf16 = jax.ShapeDtypeStruct((n, d), BF16)
    if position_minor:
        assert seq_len % tm == 0
        tiles = seq_len // tm
        state_spec = pl.BlockSpec((None, N_HEADS, HEAD_DIM, tm), lambda i: (i // tiles, 0, 0, i % tiles))
        state_f32 = jax.ShapeDtypeStruct((n // seq_len, N_HEADS, HEAD_DIM, seq_len), F32)
        scratch, state_bytes = [], tm * d * 4
    else:
        state_spec = pl.BlockSpec((tm // sub, sub, N_HEADS, HEAD_DIM), lambda i: (i, 0, 0, 0))
        state_f32 = jax.ShapeDtypeStruct((n // sub, sub, N_HEADS, HEAD_DIM), F32)
        scratch = [pltpu.VMEM((tm // sub, d // V7X_LANES * sub, V7X_LANES), F32)]
        state_bytes = tm * N_HEADS * V7X_LANES * 4
    vmem = w_qkv.size * 2 + 2 * tm * d * 4 + 2 * 3 * tm * d * 2 + 2 * 2 * state_bytes + 6 * tm * d * 4
    return pl.pallas_call(
        functools.partial(_qkv_proj_kernel, position_minor=position_minor),
        grid=(n // tm,),
        in_specs=[tok_spec, _const_spec(g.shape), _const_spec(w_qkv.shape)],
        out_specs=(tok_spec, tok_spec, tok_spec, state_spec, state_spec),
        out_shape=(tok_bf16, tok_bf16, tok_bf16, state_f32, state_f32),
        scratch_shapes=scratch,
        compiler_params=_params(("arbitrary",), vmem),
        name="qkv_proj",
    )(x2d, g, w_qkv)


def _gate_proj_kernel(x_ref, g_ref, w_ref, wf_ref, bf_ref,
                      xr_ref, yr_ref, ga_ref, gb_ref, lf_ref, tail_ref, *, seg_len):
    hn = _rmsnorm(x_ref[...], g_ref[...]).astype(BF16)
    tm = hn.shape[0]
    for s, o_ref in enumerate((xr_ref, yr_ref, ga_ref, gb_ref)):
        r = jnp.dot(hn, w_ref[s], preferred_element_type=F32)
        o_ref[...] = r.astype(o_ref.dtype)
        if o_ref is xr_ref:
            for j in range(tm // seg_len):
                end = (j + 1) * seg_len
                tail_ref[j] = r[end - V7X_SUBLANES:end, :]
    zf = jnp.dot(hn, wf_ref[...], preferred_element_type=F32)
    lf_ref[...] = _log_sigmoid(zf + bf_ref[...])[:, :N_HEADS]


def _gate_proj(x2d, g, w_gate, wf, bf, seq_len):
    n, d = x2d.shape
    tm = min(TOKEN_TILE, n)
    assert n % tm == 0 and (seq_len % tm == 0 or tm % seq_len == 0)
    seg_len = min(seq_len, tm)
    nseg = tm // seg_len
    tiles_per_stream = max(1, seq_len // tm)
    row = lambda i: (i, 0)
    tok_bf16 = jax.ShapeDtypeStruct((n, d), BF16)
    tok_spec = pl.BlockSpec((tm, d), row)
    out_shape = (tok_bf16,) * 4 + (jax.ShapeDtypeStruct((n, N_HEADS), F32),
                                   jax.ShapeDtypeStruct((n // seq_len, V7X_SUBLANES, d), F32))
    out_specs = (tok_spec,) * 4 + (
        pl.BlockSpec((tm, N_HEADS), row),
        pl.BlockSpec((nseg, V7X_SUBLANES, d), lambda i: (i // tiles_per_stream, 0, 0)))
    vmem = (w_gate.size + wf.size) * 2 + 2 * tm * d * 4 + 2 * 4 * tm * d * 2 + 5 * tm * d * 4
    return pl.pallas_call(
        functools.partial(_gate_proj_kernel, seg_len=seg_len),
        grid=(n // tm,),
        in_specs=[tok_spec, _const_spec(g.shape), _const_spec(w_gate.shape),
                  _const_spec(wf.shape), _const_spec(bf.shape)],
        out_specs=out_specs,
        out_shape=out_shape,
        compiler_params=_params(("arbitrary",), vmem),
        name="gate_proj",
    )(x2d, g, w_gate, wf, bf)


def _cumsum_kernel(lf_ref, c_ref):
    tk = lf_ref.shape[0]
    blk = min(CUMSUM_BLOCK, tk)
    row = lax.broadcasted_iota(jnp.int32, (blk, blk), 0)
    col = lax.broadcasted_iota(jnp.int32, (blk, blk), 1)
    lower = (row >= col).astype(BF16)
    carry = jnp.zeros((1, N_HEADS), F32)
    for s in range(0, tk, blk):
        sz = min(blk, tk - s)
        c = carry
        for part in _split3(lf_ref[s:s + sz, :]):
            c = c + jnp.dot(lower[:sz, :sz], part.astype(BF16), preferred_element_type=F32)
        c_ref[s:s + sz, :] = c * LOG2_E
        carry = c[sz - 1:sz, :]


def _cumsum(lf):
    b, tk, h = lf.shape
    spec = pl.BlockSpec((None, tk, h), lambda i: (i, 0, 0))
    return pl.pallas_call(
        _cumsum_kernel,
        grid=(b,),
        in_specs=[spec],
        out_specs=spec,
        out_shape=jax.ShapeDtypeStruct((b, tk, h), F32),
        compiler_params=_params(("arbitrary",), 8 * tk * V7X_LANES * 4),
        name="logf_cumsum",
    )(lf)


def _augment(own, lane, base, data, ones_first, parts):
    aug = jnp.zeros(lane.shape, F32)
    for i, part in enumerate(parts):
        at = base + i + (N_SPLIT if ones_first else 0)
        aug = jnp.where(lane == at, part, aug)
    ones_at = base if ones_first else base + N_SPLIT
    aug = jnp.where((lane >= ones_at) & (lane < ones_at + N_SPLIT), 1.0, aug)
    return aug if data is None else jnp.where(own, data.astype(F32), aug)


def _attn_first_kernel(q_ref, k_ref, v_ref, c_ref, o_ref, kt_sc, v_sc, *, tq):
    pair = pl.program_id(1)
    qi = pl.program_id(2)
    n_chunks = k_ref.shape[0] // tq
    lane = lax.broadcasted_iota(jnp.int32, (tq, V7X_LANES), 1)
    own = (lane < HEAD_DIM, lane >= HEAD_DIM)
    base = (HEAD_DIM, 0)

    def head_columns(c_rows):
        head = lax.broadcasted_iota(jnp.int32, c_rows.shape, 1)
        return [jnp.sum(jnp.where(head == _head_of(HEADS_NATURAL, pair, h), c_rows, 0.0), axis=1, keepdims=True)
                for h in range(HEADS_PER_TILE)]

    def stage_keys():
        for j in range(n_chunks):
            rows = slice(j * tq, (j + 1) * tq)
            k = k_ref[rows, :]
            v = v_ref[rows, :]
            c_cols = head_columns(c_ref[rows, :])
            for h in range(HEADS_PER_TILE):
                neg_parts = [-part for part in _split3(c_cols[h])]
                k_wide = _augment(own[h], lane, base[h], k, False, neg_parts)
                kt_sc[h, j] = k_wide.T.astype(BF16)
                v_sc[h, rows, :] = jnp.where(own[h], v.astype(F32),
                                             jnp.where(lane == base[h], 1.0, 0.0)).astype(BF16)

    causal = (lax.broadcasted_iota(jnp.int32, (tq, tq), 1)
              <= lax.broadcasted_iota(jnp.int32, (tq, tq), 0))

    def query_tile(n):
        if n == 0:
            stage_keys()
        q = q_ref[...]
        cq_cols = head_columns(c_ref[n * tq:(n + 1) * tq, :])
        for h in range(HEADS_PER_TILE):
            q_wide = _augment(own[h], lane, base[h], q, True, _split3(cq_cols[h])).astype(BF16)
            s = jnp.dot(q_wide, kt_sc[h, n], preferred_element_type=F32)
            s = jnp.where(causal, s, -jnp.inf)
            m = jnp.max(s, axis=1, keepdims=True)
            p = jnp.exp2(s - m).astype(BF16)
            acc = jnp.dot(p, v_sc[h, n * tq:(n + 1) * tq, :], preferred_element_type=F32)
            for j in range(n):
                s = jnp.dot(q_wide, kt_sc[h, j], preferred_element_type=F32)
                m_new = jnp.maximum(m, jnp.max(s, axis=1, keepdims=True))
                p = jnp.exp2(s - m_new).astype(BF16)
                acc = (jnp.exp2(m - m_new) * acc
                       + jnp.dot(p, v_sc[h, j * tq:(j + 1) * tq, :], preferred_element_type=F32))
                m = m_new
            out = acc / acc[:, base[h]:base[h] + 1]
            lanes_h = slice(h * HEAD_DIM, (h + 1) * HEAD_DIM)
            o_ref[:, lanes_h] = out[:, lanes_h].astype(o_ref.dtype)

    for n in range(n_chunks):
        pl.when(qi == n)(functools.partial(query_tile, n))


def _attention_first(q, k, v, c):
    b, t, d = q.shape
    tq = min(ATTN_Q_TILE, t)
    assert t % tq == 0
    lanes = V7X_LANES
    seq_spec = pl.BlockSpec((None, t, lanes), lambda i, p, j: (i, 0, p))
    q_spec = pl.BlockSpec((None, tq, lanes), lambda i, p, j: (i, j, p))
    vmem = (4 * t * lanes * 2 + 2 * t * lanes * 4 + 4 * t * lanes * 2 + 10 * tq * tq * 4
            + 8 * tq * lanes * 4)
    return pl.pallas_call(
        functools.partial(_attn_first_kernel, tq=tq),
        grid=(b, N_PAIRS, t // tq),
        in_specs=[q_spec, seq_spec, seq_spec, pl.BlockSpec((None, t, N_HEADS), lambda i, p, j: (i, 0, 0))],
        out_specs=q_spec,
        out_shape=jax.ShapeDtypeStruct((b, t, d), BF16),
        scratch_shapes=[pltpu.VMEM((HEADS_PER_TILE, t // tq, lanes, tq), BF16),
                        pltpu.VMEM((HEADS_PER_TILE, t, lanes), BF16)],
        compiler_params=_params(("arbitrary", "arbitrary", "arbitrary"), vmem),
        name="fox_attention_first",
    )(q, k, v, c)


def _attn_hist_kernel(q_ref, kn_ref, vn_ref, kh_ref, vh_ref, c_ref, o_ref, k_sc, v_sc, *, past):
    t = q_ref.shape[0]
    tk = past + t
    lanes = V7X_LANES
    lane = lax.broadcasted_iota(jnp.int32, (tk, lanes), 1)
    lane_q = lax.broadcasted_iota(jnp.int32, (t, lanes), 1)
    sq_row = lax.broadcasted_iota(jnp.int32, (lanes, lanes), 0)
    sq_col = lax.broadcasted_iota(jnp.int32, (lanes, lanes), 1)
    to_low = [((sq_row == sq_col + s * HEAD_DIM) & (sq_col < HEAD_DIM)).astype(BF16)
              for s in range(HEADS_PER_TILE)]
    to_slot = [((sq_col == sq_row + s * HEAD_DIM) & (sq_row < HEAD_DIM)).astype(BF16)
               for s in range(HEADS_PER_TILE)]
    causal = (lax.broadcasted_iota(jnp.int32, (t, lanes), 1)
              <= lax.broadcasted_iota(jnp.int32, (t, lanes), 0))
    key_parts = _split3(c_ref[...])
    query_parts = _split3(c_ref[past:tk, :])
    ones_col = jnp.where(lane_q == HEAD_DIM, 1.0, 0.0).astype(BF16)
    k_sc[tk:k_sc.shape[0], :] = jnp.zeros((k_sc.shape[0] - tk, lanes), BF16)
    v_sc[t:lanes, :] = jnp.zeros((lanes - t, lanes), BF16)
    nt = (((1,), (1,)), ((), ()))
    o_tiles = [jnp.zeros((t, lanes), F32) for _ in range(N_PAIRS)]

    for cached_pair in range(N_PAIRS):
        rows = slice(cached_pair * HEADS_PER_TILE, (cached_pair + 1) * HEADS_PER_TILE)
        kt_pair = kh_ref[rows].reshape(lanes, past).astype(BF16)
        vt_pair = vh_ref[rows].reshape(lanes, past).astype(BF16)
        for cached_slot in range(HEADS_PER_TILE):
            head = _head_of(HEADS_NATURAL, cached_pair, cached_slot)
            pair, slot = head % N_PAIRS, head // N_PAIRS
            cols = slice(pair * lanes, (pair + 1) * lanes)
            low = lambda a: jnp.dot(a, to_low[slot], preferred_element_type=F32)
            neg_parts = [-part[:, head:head + 1] for part in key_parts]
            k_sc[0:tk, :] = _augment(None, lane, HEAD_DIM, None, False, neg_parts).astype(BF16)
            k_sc[past:tk, 0:HEAD_DIM] = low(kn_ref[:, cols])[:, 0:HEAD_DIM].astype(BF16)
            v_sc[0:t, :] = ones_col
            v_sc[0:t, 0:HEAD_DIM] = low(vn_ref[:, cols])[:, 0:HEAD_DIM].astype(BF16)
            q_low = low(q_ref[:, cols])
            q_wide = _augment(lane_q < HEAD_DIM, lane_q, HEAD_DIM, q_low, True,
                              [part[:, head:head + 1] for part in query_parts]).astype(BF16)
            q_cached = jnp.dot(q_low.astype(BF16), to_slot[cached_slot],
                               preferred_element_type=F32).astype(BF16)
            s_hist = (jnp.dot(q_cached, kt_pair, preferred_element_type=F32)
                      + lax.dot_general(q_wide, k_sc[0:past, :], nt, preferred_element_type=F32))
            s_new = lax.dot_general(q_wide, k_sc[past:past + lanes, :], nt, preferred_element_type=F32)
            s_new = jnp.where(causal, s_new, -jnp.inf)
            m = jnp.maximum(jnp.max(s_hist, axis=1, keepdims=True), jnp.max(s_new, axis=1, keepdims=True))
            p_hist = jnp.exp2(s_hist - m).astype(BF16)
            acc_new = jnp.dot(jnp.exp2(s_new - m).astype(BF16), v_sc[...], preferred_element_type=F32)
            o_hist = lax.dot_general(p_hist, vt_pair, nt, preferred_element_type=F32)
            if cached_slot:
                o_hist = pltpu.roll(o_hist, HEAD_DIM, 1)
            denom = jnp.sum(p_hist.astype(F32), axis=1, keepdims=True) + acc_new[:, HEAD_DIM:HEAD_DIM + 1]
            o_head = ((o_hist + acc_new) / denom).astype(BF16)
            o_tiles[pair] = o_tiles[pair] + jnp.dot(o_head, to_slot[slot], preferred_element_type=F32)
    for pair in range(N_PAIRS):
        o_ref[:, pair * lanes:(pair + 1) * lanes] = o_tiles[pair].astype(o_ref.dtype)


def _attention_hist(q, k_new, v_new, k_hist, v_hist, c):
    b, t, d = q.shape
    past = k_hist.shape[1]
    lanes = V7X_LANES
    assert t <= lanes and t % V7X_SUBLANES == 0 and past % lanes == 0
    k_t = jnp.transpose(k_hist, (0, 2, 3, 1))
    v_t = jnp.transpose(v_hist, (0, 2, 3, 1))
    new_spec = pl.BlockSpec((None, t, d), lambda i: (i, 0, 0))
    hist_spec = pl.BlockSpec((None, N_HEADS, HEAD_DIM, past), lambda i: (i, 0, 0, 0))
    vmem = (2 * 2 * past * d * 4 + 8 * t * d * 2 + 2 * (past + t) * lanes * 4
            + (past + 2 * lanes) * lanes * 2 + 4 * past * lanes * 2 + 16 * (past + t) * lanes * 4)
    return pl.pallas_call(
        functools.partial(_attn_hist_kernel, past=past),
        grid=(b,),
        in_specs=[new_spec, new_spec, new_spec, hist_spec, hist_spec,
                  pl.BlockSpec((None, past + t, N_HEADS), lambda i: (i, 0, 0))],
        out_specs=new_spec,
        out_shape=jax.ShapeDtypeStruct((b, t, d), BF16),
        scratch_shapes=[pltpu.VMEM((past + lanes, lanes), BF16), pltpu.VMEM((lanes, lanes), BF16)],
        compiler_params=_params(("arbitrary",), vmem),
        name="fox_attention_hist",
    )(q, k_new, v_new, k_t, v_t, c)


def _neg_expm1(y, exp_half_y):
    series = y * (-1.0 / 120.0) - 1.0 / 24.0
    for coeff in (-1.0 / 6.0, -0.5, -1.0):
        series = series * y + coeff
    return jnp.where(y > -0.0625, y * series, 1.0 - exp_half_y * exp_half_y)


def _scan_groups(a, b):
    rows, width = a.shape
    sub = V7X_SUBLANES
    a = a.reshape(rows // sub, sub, width)
    b = b.reshape(rows // sub, sub, width)
    row = lax.broadcasted_iota(jnp.int32, a.shape, 1)
    for shift in (1, 2, 4):
        keep = row >= shift
        a_prev = jnp.where(keep, pltpu.roll(a, shift, 1), 1.0)
        b_prev = jnp.where(keep, pltpu.roll(b, shift, 1), 0.0)
        b = a * b_prev + b
        a = a * a_prev
    return a.reshape(rows, width), b.reshape(rows, width)


def _rnn_tile(xr, yr, weight_refs, state_scratch):
    cw_ref, cb_ref, wg_ref, ba_ref, bx_ref, lam_ref = weight_refs
    ext_sc, a_sc, b_sc, h_sc, hc_sc = state_scratch
    sub = V7X_SUBLANES
    tt, d = xr.shape
    ext_sc[sub:sub + tt, :] = xr
    cw = cw_ref[...]
    xc = ext_sc[sub - 3:sub - 3 + tt, :] * cw[0:1, :]
    for w in range(1, CONV_W):
        xc = xc + ext_sc[sub - 3 + w:sub - 3 + w + tt, :] * cw[w:w + 1, :]
    xc = cb_ref[...] + xc
    ext_sc[0:sub, :] = ext_sc[tt:tt + sub, :]

    xcb = xc.astype(BF16)
    wb = V7X_MXU_DIM
    for j in range(d // wb):
        cols = slice(j * wb, (j + 1) * wb)
        xcj = xc[:, cols]
        g = jnp.dot(xcb[:, cols], wg_ref[j], preferred_element_type=F32)
        r = jax.nn.sigmoid(g[:, :wb] + ba_ref[:, cols])
        i = jax.nn.sigmoid(g[:, wb:] + bx_ref[:, cols])
        log_a = (-RG_C * r) * _softplus(-lam_ref[:, cols])
        a = jnp.exp(log_a)
        bx = jnp.sqrt(_neg_expm1(2.0 * log_a, a)) * (i * xcj)
        a_sc[:, cols], b_sc[:, cols] = _scan_groups(a, bx)

    hc = hc_sc[...]
    for grp in range(tt // sub):
        rows = slice(grp * sub, (grp + 1) * sub)
        h = a_sc[rows, :] * hc + b_sc[rows, :]
        h_sc[rows, :] = h
        hc = h[sub - 1:sub, :]
    hc_sc[...] = hc
    return h_sc[...] * _gelu_tanh(yr)


def _init_rnn_state(state_scratch, conv_rows, h0):
    ext_sc, _, _, _, hc_sc = state_scratch
    ext_sc[0:V7X_SUBLANES, :] = conv_rows
    hc_sc[...] = h0


def _rnn_scratch(tt, d):
    return [pltpu.VMEM((tt + V7X_SUBLANES, d), F32), pltpu.VMEM((tt, d), F32),
            pltpu.VMEM((tt, d), F32), pltpu.VMEM((tt, d), F32), pltpu.VMEM((1, d), F32)]


def _rglru_kernel(*refs, has_state):
    if has_state:
        xr_ref, yr_ref, ch_ref, h0_ref = refs[:4]
        refs = refs[4:]
    else:
        xr_ref, yr_ref = refs[:2]
        refs = refs[2:]
    weight_refs, (o_ref, hl_ref), state_scratch = refs[:6], refs[6:8], refs[8:]
    d = xr_ref.shape[1]

    @pl.when(pl.program_id(1) == 0)
    def _first_tile():
        if has_state:
            _init_rnn_state(state_scratch, ch_ref[...], h0_ref[...])
        else:
            _init_rnn_state(state_scratch, jnp.zeros((V7X_SUBLANES, d), F32), jnp.zeros((1, d), F32))

    out = _rnn_tile(xr_ref[...].astype(F32), yr_ref[...].astype(F32), weight_refs, state_scratch)
    o_ref[...] = out.astype(o_ref.dtype)
    hl_ref[...] = state_scratch[4][...]


def _rglru(xr, yr, conv_hist, h0, cw, cb, wg, ba, bx, lam):
    b, t, d = xr.shape
    tt = min(RNN_TIME_TILE, t)
    assert t % tt == 0 and tt % V7X_SUBLANES == 0
    has_state = conv_hist is not None
    tok_spec = pl.BlockSpec((None, tt, d), lambda i, j: (i, j, 0))
    in_specs = [tok_spec, tok_spec]
    args = [xr, yr]
    if has_state:
        in_specs += [pl.BlockSpec((None, V7X_SUBLANES, d), lambda i, j: (i, 0, 0)),
                     pl.BlockSpec((None, 1, d), lambda i, j: (i, 0, 0))]
        args += [conv_hist, h0]
    weights = [cw, cb, wg, ba, bx, lam]
    in_specs += [_const_spec(w.shape) for w in weights]
    args += weights
    vmem = 4 * tt * d * 2 * 2 + 2 * tt * d * 2 + 4 * (tt + 8) * d * 4 + 16 * tt * d * 4
    return pl.pallas_call(
        functools.partial(_rglru_kernel, has_state=has_state),
        grid=(b, t // tt),
        in_specs=in_specs,
        out_specs=(tok_spec, pl.BlockSpec((None, 1, d), lambda i, j: (i, 0, 0))),
        out_shape=(jax.ShapeDtypeStruct((b, t, d), BF16), jax.ShapeDtypeStruct((b, 1, d), F32)),
        scratch_shapes=_rnn_scratch(tt, d),
        compiler_params=_params(("arbitrary", "arbitrary"), vmem),
        name="rglru",
    )(*args)


def _gate_rnn_kernel(x_ref, g_ref, w_ref, wf_ref, bf_ref, cw_ref, cb_ref, wg_ref, ba_ref, bx_ref, lam_ref,
                     o_ref, ga_ref, gb_ref, lf_ref, tail_ref, hl_ref, *state_scratch, tiles_per_stream):
    tm, d = x_ref.shape

    @pl.when(pl.program_id(0) % tiles_per_stream == 0)
    def _first_tile():
        _init_rnn_state(state_scratch, jnp.zeros((V7X_SUBLANES, d), F32), jnp.zeros((1, d), F32))

    hn = _rmsnorm(x_ref[...], g_ref[...]).astype(BF16)
    xr = jnp.dot(hn, w_ref[0], preferred_element_type=F32)
    tail_ref[0] = xr[tm - V7X_SUBLANES:tm, :]
    yr = jnp.dot(hn, w_ref[1], preferred_element_type=F32)
    ga_ref[...] = jnp.dot(hn, w_ref[2], preferred_element_type=F32).astype(ga_ref.dtype)
    gb_ref[...] = jnp.dot(hn, w_ref[3], preferred_element_type=F32).astype(gb_ref.dtype)
    zf = jnp.dot(hn, wf_ref[...], preferred_element_type=F32)
    lf_ref[...] = _log_sigmoid(zf + bf_ref[...])[:, :N_HEADS]
    weight_refs = (cw_ref, cb_ref, wg_ref, ba_ref, bx_ref, lam_ref)
    o_ref[...] = _rnn_tile(xr, yr, weight_refs, state_scratch).astype(o_ref.dtype)
    hl_ref[...] = state_scratch[4][...]


def _gate_rnn_proj(x2d, g, w_gate, wf, bf, rnn_weights, seq_len):
    n, d = x2d.shape
    tm = min(TOKEN_TILE, n)
    assert n % tm == 0 and seq_len % tm == 0
    tiles_per_stream = seq_len // tm
    n_streams = n // seq_len
    row = lambda i: (i, 0)
    stream = lambda i: (i // tiles_per_stream, 0, 0)
    tok_bf16 = jax.ShapeDtypeStruct((n, d), BF16)
    tok_spec = pl.BlockSpec((tm, d), row)
    weights = [g, w_gate, wf, bf, *rnn_weights]
    vmem = (sum(w.size * w.dtype.itemsize for w in weights) + 2 * tm * d * 4 + 2 * 3 * tm * d * 2
            + 4 * (tm + 8) * d * 4 + 14 * tm * d * 4)
    return pl.pallas_call(
        functools.partial(_gate_rnn_kernel, tiles_per_stream=tiles_per_stream),
        grid=(n // tm,),
        in_specs=[tok_spec] + [_const_spec(w.shape) for w in weights],
        out_specs=(tok_spec, tok_spec, tok_spec, pl.BlockSpec((tm, N_HEADS), row),
                   pl.BlockSpec((1, V7X_SUBLANES, d), stream), pl.BlockSpec((None, 1, d), stream)),
        out_shape=(tok_bf16, tok_bf16, tok_bf16, jax.ShapeDtypeStruct((n, N_HEADS), F32),
                   jax.ShapeDtypeStruct((n_streams, V7X_SUBLANES, d), F32),
                   jax.ShapeDtypeStruct((n_streams, 1, d), F32)),
        scratch_shapes=_rnn_scratch(tm, d),
        compiler_params=_params(("arbitrary",), vmem),
        name="gate_rnn_proj",
    )(x2d, *weights)


def _out_mlp_kernel(x_ref, oa_ref, or_ref, ga_ref, gb_ref, wpa_ref, wpr_ref, wo_ref, g2_ref,
                    wup_ref, wdn_ref, gf_ref, y_ref):
    d = x_ref.shape[1]
    ya = jnp.dot(oa_ref[...], wpa_ref[...], preferred_element_type=F32)
    yb = jnp.dot(or_ref[...], wpr_ref[...], preferred_element_type=F32)
    merged = (jax.nn.sigmoid(ga_ref[...].astype(F32)) * ya
              + jax.nn.sigmoid(gb_ref[...].astype(F32)) * yb)
    x1 = x_ref[...] + jnp.dot(merged.astype(BF16), wo_ref[...], preferred_element_type=F32)
    h2 = _rmsnorm(x1, g2_ref[...]).astype(BF16)
    acc = x1
    for j in range(wup_ref.shape[1] // d):
        cols = slice(j * d, (j + 1) * d)
        u = jnp.maximum(jnp.dot(h2, wup_ref[:, cols], preferred_element_type=F32), 0.0)
        acc = acc + jnp.dot((u * u).astype(BF16), wdn_ref[cols, :], preferred_element_type=F32)
    y_ref[...] = _rmsnorm(acc, gf_ref[...])


def _out_mlp(x2d, oa, orn, ga, gb, wpa, wpr, wo, g2, wup, wdn, gf):
    n, d = x2d.shape
    tm = min(TOKEN_TILE, n)
    assert n % tm == 0
    tok_spec = pl.BlockSpec((tm, d), lambda i: (i, 0))
    weights = [wpa, wpr, wo, g2, wup, wdn, gf]
    vmem = (sum(w.size * w.dtype.itemsize for w in weights) + 2 * tm * d * (4 + 4 * 2 + 4)
            + 10 * tm * d * 4)
    return pl.pallas_call(
        _out_mlp_kernel,
        grid=(n // tm,),
        in_specs=[tok_spec] * 5 + [_const_spec(w.shape) for w in weights],
        out_specs=tok_spec,
        out_shape=jax.ShapeDtypeStruct((n, d), F32),
        compiler_params=_params(("arbitrary",), vmem),
        name="out_mlp",
    )(x2d, oa, orn, ga, gb, *weights)


def _prepare_weights(norm_mix_g, w_in, b_f, conv_w, conv_b, w_rg_a, b_rg_a, w_rg_x, b_rg_x, rg_lambda,
                     w_proj_attn, w_proj_rnn, w_out, norm_mlp_g, w_up, w_down, norm_final_g):
    d = w_in.shape[0]
    d_attn = N_HEADS * HEAD_DIM
    qkv_end = 3 * d_attn
    rest = qkv_end + N_HEADS
    split = np.array([_head_of(HEADS_SPLIT, p, s) * HEAD_DIM + f for p in range(N_PAIRS)
                      for s in range(HEADS_PER_TILE) for f in range(HEAD_DIM)])
    w_q = w_in[:, :d_attn] * (HEAD_DIM ** -0.5 * LOG2_E)
    w_qkv = jnp.stack([w_q, w_in[:, d_attn:2 * d_attn], w_in[:, 2 * d_attn:qkv_end]]).astype(BF16)
    w_gate = jnp.stack([w_in[:, rest + s * d:rest + (s + 1) * d] for s in range(4)]).astype(BF16)
    pad = V7X_LANES - N_HEADS
    wf = jnp.pad(w_in[:, qkv_end:rest], ((0, 0), (0, pad))).astype(BF16)
    bf = jnp.pad(b_f, (0, pad)).reshape(1, V7X_LANES)

    def block_diag(w):
        n_tiles = w.shape[0] // GATE_BLOCKS_PER_TILE
        w4 = w.reshape(n_tiles, GATE_BLOCKS_PER_TILE, RNN_BLOCK, RNN_BLOCK)
        eye = jnp.eye(GATE_BLOCKS_PER_TILE, dtype=w.dtype)
        return jnp.einsum('jmde,mn->jmdne', w4, eye).reshape(n_tiles, V7X_MXU_DIM, V7X_MXU_DIM)

    wg = jnp.concatenate([block_diag(w_rg_a), block_diag(w_rg_x)], axis=-1).astype(BF16)
    row = lambda v: v.reshape(1, -1)
    return dict(
        qkv_proj=(row(norm_mix_g), w_qkv), qkv_proj_split=(row(norm_mix_g), w_qkv[:, :, split]),
        gate_proj=(row(norm_mix_g), w_gate, wf, bf),
        rglru=(conv_w, row(conv_b), wg, row(b_rg_a), row(b_rg_x), row(rg_lambda)),
        proj_attn=w_proj_attn.astype(BF16), proj_attn_split=w_proj_attn[split, :].astype(BF16),
        out_mlp=(w_proj_rnn.astype(BF16), w_out.astype(BF16), row(norm_mlp_g),
                 w_up.astype(BF16), w_down.astype(BF16), row(norm_final_g)))


def _layer(x, k_hist, v_hist, logf_hist, conv_hist, h0, wts):
    b, t, d = x.shape
    x2d = x.reshape(b * t, d)
    first = k_hist is None
    q, kb, vb, k_state, v_state = _qkv_proj(x2d, *wts['qkv_proj' if first else 'qkv_proj_split'],
                                            seq_len=t, position_minor=first)
    as_seq = lambda a: a.reshape(b, t, d)
    if first:
        o_rnn, ga, gb, lf, tail, h_last = _gate_rnn_proj(x2d, *wts['gate_proj'], wts['rglru'], seq_len=t)
        lf_new = lf.reshape(b, t, N_HEADS)
        o_attn = _attention_first(as_seq(q), as_seq(kb), as_seq(vb), _cumsum(lf_new))
        new_k, new_v = (jnp.transpose(s, (0, 3, 1, 2)) for s in (k_state, v_state))
    else:
        xr, yr, ga, gb, lf, tail = _gate_proj(x2d, *wts['gate_proj'], seq_len=t)
        lf_new = lf.reshape(b, t, N_HEADS)
        c = _cumsum(jnp.concatenate([logf_hist, lf_new], axis=1))
        o_attn = _attention_hist(as_seq(q), as_seq(kb), as_seq(vb), k_hist, v_hist, c)
        new_k, new_v = (s.reshape(b, t, N_HEADS, HEAD_DIM) for s in (k_state, v_state))
        conv_hist = jnp.pad(conv_hist, ((0, 0), (V7X_SUBLANES - (CONV_W - 1), 0), (0, 0)))
        o_rnn, h_last = _rglru(as_seq(xr), as_seq(yr), conv_hist, h0.reshape(b, 1, d), *wts['rglru'])
    y = _out_mlp(x2d, o_attn.reshape(b * t, d), o_rnn.reshape(b * t, d), ga, gb,
                 wts['proj_attn' if first else 'proj_attn_split'], *wts['out_mlp'])
    new_conv = tail[:, V7X_SUBLANES - (CONV_W - 1):, :]
    return y.reshape(b, t, d), new_k, new_v, lf_new, new_conv, h_last.reshape(b, d)


def kernel(x_prompt, x_sample, cache_k, cache_v, cache_logf, state_conv, state_rglru, norm_mix_g, w_in, b_f, conv_w, conv_b, w_rg_a, b_rg_a, w_rg_x, b_rg_x, rg_lambda, w_proj_attn, w_proj_rnn, w_out, norm_mlp_g, w_up, w_down, norm_final_g):
    assert w_in.shape[0] == 1, "the final norm is fused into the layer: one layer only"
    wts = _prepare_weights(norm_mix_g[0], w_in[0], b_f[0], conv_w[0], conv_b[0], w_rg_a[0], b_rg_a[0],
                           w_rg_x[0], b_rg_x[0], rg_lambda[0], w_proj_attn[0], w_proj_rnn[0], w_out[0],
                           norm_mlp_g[0], w_up[0], w_down[0], norm_final_g)
    yp, kp, vp, lp, cp, hp = _layer(x_prompt, None, None, None, None, None, wts)
    ys, ks, vs, ls, cs, hs = _layer(x_sample, cache_k[0], cache_v[0], cache_logf[0], state_conv[0],
                                    state_rglru[0], wts)
    lead = lambda a: a[None]
    return (yp, ys, lead(kp), lead(vp), lead(lp), lead(cp), lead(hp),
            lead(ks), lead(vs), lead(ls), lead(cs), lead(hs))
```

```python
import functools

import numpy as np
import jax
import jax.numpy as jnp
from jax import lax
from jax.experimental import pallas as pl
from jax.experimental.pallas import tpu as pltpu

F32 = jnp.float32
BF16 = jnp.bfloat16

N_HEADS = 16
HEAD_DIM = 64
CONV_W = 4
RNN_BLOCK = 64
RG_C = 8.0
EPS = 1e-6
LOG2_E = 1.4426950408889634

V7X_LANES = 128
V7X_SUBLANES = 8
V7X_MXU_DIM = 256
V7X_VMEM_BYTES = 64 * 1024 * 1024

HEADS_PER_TILE = V7X_LANES // HEAD_DIM
N_PAIRS = N_HEADS // HEADS_PER_TILE
GATE_BLOCKS_PER_TILE = V7X_MXU_DIM // RNN_BLOCK
TOKEN_TILE = 512
ATTN_Q_TILE = 512
RNN_TIME_TILE = 256
CUMSUM_BLOCK = 256
N_SPLIT = 3


HEADS_NATURAL = (HEADS_PER_TILE, 1)
HEADS_SPLIT = (1, N_PAIRS)


def _head_of(order, pair, slot):
    return pair * order[0] + slot * order[1]


def _vmem_limit(estimate_bytes):
    return int(min(estimate_bytes + (12 << 20), V7X_VMEM_BYTES - (6 << 20)))


def _params(semantics, vmem_estimate):
    return pltpu.CompilerParams(dimension_semantics=semantics,
                                vmem_limit_bytes=_vmem_limit(vmem_estimate))


def _const_spec(shape):
    zeros = (0,) * len(shape)
    return pl.BlockSpec(shape, lambda *_: zeros, pipeline_mode=pl.Buffered(1))


def _rmsnorm(x, g):
    x = x * lax.rsqrt(jnp.mean(x * x, axis=-1, keepdims=True) + EPS)
    return x * g


def _log_sigmoid(x):
    return jnp.minimum(x, 0.0) - jnp.log1p(jnp.exp(-jnp.abs(x)))


def _softplus(x):
    return jnp.maximum(x, 0.0) + jnp.log1p(jnp.exp(-jnp.abs(x)))


def _gelu_tanh(x):
    c = 0.7978845608028654
    return x * (0.5 * (1.0 + jnp.tanh(c * (x + 0.044715 * (x * x * x)))))


def _split3(x):
    hi = x.astype(BF16).astype(F32)
    r1 = x - hi
    mid = r1.astype(BF16).astype(F32)
    lo = (r1 - mid).astype(BF16).astype(F32)
    return hi, mid, lo


def _store_head_major(r, o_ref, slab_sc):
    tm = r.shape[0]
    sub = V7X_SUBLANES
    n_tiles = r.shape[1] // V7X_LANES
    for j in range(n_tiles):
        slab_sc[:, j * sub:(j + 1) * sub, :] = (
            r[:, j * V7X_LANES:(j + 1) * V7X_LANES].reshape(tm // sub, sub, V7X_LANES))
    for t in range(sub):
        rows = slab_sc[:, pl.ds(t, n_tiles, stride=sub), :]
        o_ref[:, t, 0:N_PAIRS, :] = rows[:, :, 0:HEAD_DIM]
        o_ref[:, t, N_PAIRS:N_HEADS, :] = rows[:, :, HEAD_DIM:V7X_LANES]


def _qkv_proj_kernel(x_ref, g_ref, w_ref, q_ref, kb_ref, vb_ref, ks_ref, vs_ref, *scratch, position_minor):
    hn = _rmsnorm(x_ref[...], g_ref[...]).astype(BF16)
    q_ref[...] = jnp.dot(hn, w_ref[0], preferred_element_type=F32).astype(q_ref.dtype)
    for s, (b_ref, state_ref) in enumerate(((kb_ref, ks_ref), (vb_ref, vs_ref)), start=1):
        r = jnp.dot(hn, w_ref[s], preferred_element_type=F32)
        b_ref[...] = r.astype(b_ref.dtype)
        if position_minor:
            state_ref[...] = r.T.reshape(state_ref.shape)
        else:
            _store_head_major(r, state_ref, *scratch)


def _qkv_proj(x2d, g, w_qkv, seq_len, position_minor):
    n, d = x2d.shape
    tm = min(TOKEN_TILE, n)
    sub = V7X_SUBLANES
    assert n % tm == 0 and tm % sub == 0 and d == N_HEADS * HEAD_DIM
    tok_spec = pl.BlockSpec((tm, d), lambda i: (i, 0))
    tok_bf16 = jax.ShapeDtypeStruct((n, d), BF16)
    if position_minor:
        assert seq_len % tm == 0
        tiles = seq_len // tm
        state_spec = pl.BlockSpec((None, N_HEADS, HEAD_DIM, tm), lambda i: (i // tiles, 0, 0, i % tiles))
        state_f32 = jax.ShapeDtypeStruct((n // seq_len, N_HEADS, HEAD_DIM, seq_len), F32)
        scratch, state_bytes = [], tm * d * 4
    else:
        state_spec = pl.BlockSpec((tm // sub, sub, N_HEADS, HEAD_DIM), lambda i: (i, 0, 0, 0))
        state_f32 = jax.ShapeDtypeStruct((n // sub, sub, N_HEADS, HEAD_DIM), F32)
        scratch = [pltpu.VMEM((tm // sub, d // V7X_LANES * sub, V7X_LANES), F32)]
        state_bytes = tm * N_HEADS * V7X_LANES * 4
    vmem = w_qkv.size * 2 + 2 * tm * d * 4 + 2 * 3 * tm * d * 2 + 2 * 2 * state_bytes + 6 * tm * d * 4
    return pl.pallas_call(
        functools.partial(_qkv_proj_kernel, position_minor=position_minor),
        grid=(n // tm,),
        in_specs=[tok_spec, _const_spec(g.shape), _const_spec(w_qkv.shape)],
        out_specs=(tok_spec, tok_spec, tok_spec, state_spec, state_spec),
        out_shape=(tok_bf16, tok_bf16, tok_bf16, state_f32, state_f32),
        scratch_shapes=scratch,
        compiler_params=_params(("arbitrary",), vmem),
        name="qkv_proj",
    )(x2d, g, w_qkv)


def _gate_proj_kernel(x_ref, g_ref, w_ref, wf_ref, bf_ref,
                      xr_ref, yr_ref, ga_ref, gb_ref, lf_ref, tail_ref, *, seg_len):
    hn = _rmsnorm(x_ref[...], g_ref[...]).astype(BF16)
    tm = hn.shape[0]
    for s, o_ref in enumerate((xr_ref, yr_ref, ga_ref, gb_ref)):
        r = jnp.dot(hn, w_ref[s], preferred_element_type=F32)
        o_ref[...] = r.astype(o_ref.dtype)
        if o_ref is xr_ref:
            for j in range(tm // seg_len):
                end = (j + 1) * seg_len
                tail_ref[j] = r[end - V7X_SUBLANES:end, :]
    zf = jnp.dot(hn, wf_ref[...], preferred_element_type=F32)
    lf_ref[...] = _log_sigmoid(zf + bf_ref[...])[:, :N_HEADS]


def _gate_proj(x2d, g, w_gate, wf, bf, seq_len):
    n, d = x2d.shape
    tm = min(TOKEN_TILE, n)
    assert n % tm == 0 and (seq_len % tm == 0 or tm % seq_len == 0)
    seg_len = min(seq_len, tm)
    nseg = tm // seg_len
    tiles_per_stream = max(1, seq_len // tm)
    row = lambda i: (i, 0)
    tok_bf16 = jax.ShapeDtypeStruct((n, d), BF16)
    tok_spec = pl.BlockSpec((tm, d), row)
    out_shape = (tok_bf16,) * 4 + (jax.ShapeDtypeStruct((n, N_HEADS), F32),
                                   jax.ShapeDtypeStruct((n // seq_len, V7X_SUBLANES, d), F32))
    out_specs = (tok_spec,) * 4 + (
        pl.BlockSpec((tm, N_HEADS), row),
        pl.BlockSpec((nseg, V7X_SUBLANES, d), lambda i: (i // tiles_per_stream, 0, 0)))
    vmem = (w_gate.size + wf.size) * 2 + 2 * tm * d * 4 + 2 * 4 * tm * d * 2 + 5 * tm * d * 4
    return pl.pallas_call(
        functools.partial(_gate_proj_kernel, seg_len=seg_len),
        grid=(n // tm,),
        in_specs=[tok_spec, _const_spec(g.shape), _const_spec(w_gate.shape),
                  _const_spec(wf.shape), _const_spec(bf.shape)],
        out_specs=out_specs,
        out_shape=out_shape,
        compiler_params=_params(("arbitrary",), vmem),
        name="gate_proj",
    )(x2d, g, w_gate, wf, bf)


def _cumsum_kernel(lf_ref, c_ref):
    tk = lf_ref.shape[0]
    blk = min(CUMSUM_BLOCK, tk)
    row = lax.broadcasted_iota(jnp.int32, (blk, blk), 0)
    col = lax.broadcasted_iota(jnp.int32, (blk, blk), 1)
    lower = (row >= col).astype(BF16)
    carry = jnp.zeros((1, N_HEADS), F32)
    for s in range(0, tk, blk):
        sz = min(blk, tk - s)
        c = carry
        for part in _split3(lf_ref[s:s + sz, :]):
            c = c + jnp.dot(lower[:sz, :sz], part.astype(BF16), preferred_element_type=F32)
        c_ref[s:s + sz, :] = c * LOG2_E
        carry = c[sz - 1:sz, :]


def _cumsum(lf):
    b, tk, h = lf.shape
    spec = pl.BlockSpec((None, tk, h), lambda i: (i, 0, 0))
    return pl.pallas_call(
        _cumsum_kernel,
        grid=(b,),
        in_specs=[spec],
        out_specs=spec,
        out_shape=jax.ShapeDtypeStruct((b, tk, h), F32),
        compiler_params=_params(("arbitrary",), 8 * tk * V7X_LANES * 4),
        name="logf_cumsum",
    )(lf)


def _augment(own, lane, base, data, ones_first, parts):
    aug = jnp.zeros(lane.shape, F32)
    for i, part in enumerate(parts):
        at = base + i + (N_SPLIT if ones_first else 0)
        aug = jnp.where(lane == at, part, aug)
    ones_at = base if ones_first else base + N_SPLIT
    aug = jnp.where((lane >= ones_at) & (lane < ones_at + N_SPLIT), 1.0, aug)
    return aug if data is None else jnp.where(own, data.astype(F32), aug)


def _attn_first_kernel(q_ref, k_ref, v_ref, c_ref, o_ref, kt_sc, v_sc, *, tq):
    pair = pl.program_id(1)
    qi = pl.program_id(2)
    n_chunks = k_ref.shape[0] // tq
    lane = lax.broadcasted_iota(jnp.int32, (tq, V7X_LANES), 1)
    own = (lane < HEAD_DIM, lane >= HEAD_DIM)
    base = (HEAD_DIM, 0)

    def head_columns(c_rows):
        head = lax.broadcasted_iota(jnp.int32, c_rows.shape, 1)
        return [jnp.sum(jnp.where(head == _head_of(HEADS_NATURAL, pair, h), c_rows, 0.0), axis=1, keepdims=True)
                for h in range(HEADS_PER_TILE)]

    ones_col = [jnp.where(lane == base[h], 1.0, 0.0).astype(BF16) for h in range(HEADS_PER_TILE)]

    def stage_keys():
        for j in range(n_chunks):
            rows = slice(j * tq, (j + 1) * tq)
            k = k_ref[rows, :]
            v = v_ref[rows, :]
            c_cols = head_columns(c_ref[rows, :])
            for h in range(HEADS_PER_TILE):
                neg_parts = [-part for part in _split3(c_cols[h])]
                aug = _augment(None, lane, base[h], None, False, neg_parts).astype(BF16)
                kt_sc[h, j] = jnp.where(own[h], k, aug).T
                v_sc[h, rows, :] = jnp.where(own[h], v, ones_col[h])

    causal = (lax.broadcasted_iota(jnp.int32, (tq, tq), 1)
              <= lax.broadcasted_iota(jnp.int32, (tq, tq), 0))

    def query_tile(n):
        if n == 0:
            stage_keys()
        q = q_ref[...]
        cq_cols = head_columns(c_ref[n * tq:(n + 1) * tq, :])
        for h in range(HEADS_PER_TILE):
            q_wide = jnp.where(own[h], q, _augment(None, lane, base[h], None, True,
                                                   _split3(cq_cols[h])).astype(BF16))
            s = jnp.dot(q_wide, kt_sc[h, n], preferred_element_type=F32)
            s = jnp.where(causal, s, -jnp.inf)
            m = jnp.max(s, axis=1, keepdims=True)
            p = jnp.exp2(s - m).astype(BF16)
            acc = jnp.dot(p, v_sc[h, n * tq:(n + 1) * tq, :], preferred_element_type=F32)
            for j in range(n):
                s = jnp.dot(q_wide, kt_sc[h, j], preferred_element_type=F32)
                m_new = jnp.maximum(m, jnp.max(s, axis=1, keepdims=True))
                p = jnp.exp2(s - m_new).astype(BF16)
                acc = (jnp.exp2(m - m_new) * acc
                       + jnp.dot(p, v_sc[h, j * tq:(j + 1) * tq, :], preferred_element_type=F32))
                m = m_new
            out = acc / acc[:, base[h]:base[h] + 1]
            lanes_h = slice(h * HEAD_DIM, (h + 1) * HEAD_DIM)
            o_ref[:, lanes_h] = out[:, lanes_h].astype(o_ref.dtype)

    for n in range(n_chunks):
        pl.when(qi == n)(functools.partial(query_tile, n))


def _attention_first(q, k, v, c):
    b, t, d = q.shape
    tq = min(ATTN_Q_TILE, t)
    assert t % tq == 0
    lanes = V7X_LANES
    seq_spec = pl.BlockSpec((None, t, lanes), lambda i, p, j: (i, 0, p))
    q_spec = pl.BlockSpec((None, tq, lanes), lambda i, p, j: (i, j, p))
    vmem = (4 * t * lanes * 2 + 2 * t * lanes * 4 + 4 * t * lanes * 2 + 10 * tq * tq * 4
            + 8 * tq * lanes * 4)
    return pl.pallas_call(
        functools.partial(_attn_first_kernel, tq=tq),
        grid=(b, N_PAIRS, t // tq),
        in_specs=[q_spec, seq_spec, seq_spec, pl.BlockSpec((None, t, N_HEADS), lambda i, p, j: (i, 0, 0))],
        out_specs=q_spec,
        out_shape=jax.ShapeDtypeStruct((b, t, d), BF16),
        scratch_shapes=[pltpu.VMEM((HEADS_PER_TILE, t // tq, lanes, tq), BF16),
                        pltpu.VMEM((HEADS_PER_TILE, t, lanes), BF16)],
        compiler_params=_params(("arbitrary", "arbitrary", "arbitrary"), vmem),
        name="fox_attention_first",
    )(q, k, v, c)


def _attn_hist_kernel(q_ref, kn_ref, vn_ref, kh_ref, vh_ref, c_ref, o_ref, k_sc, v_sc, *, past):
    t = q_ref.shape[0]
    tk = past + t
    lanes = V7X_LANES
    lane = lax.broadcasted_iota(jnp.int32, (tk, lanes), 1)
    lane_q = lax.broadcasted_iota(jnp.int32, (t, lanes), 1)
    sq_row = lax.broadcasted_iota(jnp.int32, (lanes, lanes), 0)
    sq_col = lax.broadcasted_iota(jnp.int32, (lanes, lanes), 1)
    to_low = [((sq_row == sq_col + s * HEAD_DIM) & (sq_col < HEAD_DIM)).astype(BF16)
              for s in range(HEADS_PER_TILE)]
    to_slot = [((sq_col == sq_row + s * HEAD_DIM) & (sq_row < HEAD_DIM)).astype(BF16)
               for s in range(HEADS_PER_TILE)]
    causal = (lax.broadcasted_iota(jnp.int32, (t, lanes), 1)
              <= lax.broadcasted_iota(jnp.int32, (t, lanes), 0))
    key_parts = _split3(c_ref[...])
    query_parts = _split3(c_ref[past:tk, :])
    ones_col = jnp.where(lane_q == HEAD_DIM, 1.0, 0.0).astype(BF16)
    k_sc[tk:k_sc.shape[0], :] = jnp.zeros((k_sc.shape[0] - tk, lanes), BF16)
    v_sc[t:lanes, :] = jnp.zeros((lanes - t, lanes), BF16)
    nt = (((1,), (1,)), ((), ()))
    o_tiles = [jnp.zeros((t, lanes), F32) for _ in range(N_PAIRS)]

    for cached_pair in range(N_PAIRS):
        rows = slice(cached_pair * HEADS_PER_TILE, (cached_pair + 1) * HEADS_PER_TILE)
        kt_pair = kh_ref[rows].reshape(lanes, past).astype(BF16)
        vt_pair = vh_ref[rows].reshape(lanes, past).astype(BF16)
        for cached_slot in range(HEADS_PER_TILE):
            head = _head_of(HEADS_NATURAL, cached_pair, cached_slot)
            pair, slot = head % N_PAIRS, head // N_PAIRS
            cols = slice(pair * lanes, (pair + 1) * lanes)
            low = lambda a: jnp.dot(a, to_low[slot], preferred_element_type=F32)
            neg_parts = [-part[:, head:head + 1] for part in key_parts]
            k_sc[0:tk, :] = _augment(None, lane, HEAD_DIM, None, False, neg_parts).astype(BF16)
            k_sc[past:tk, 0:HEAD_DIM] = low(kn_ref[:, cols])[:, 0:HEAD_DIM].astype(BF16)
            v_sc[0:t, :] = ones_col
            v_sc[0:t, 0:HEAD_DIM] = low(vn_ref[:, cols])[:, 0:HEAD_DIM].astype(BF16)
            q_low = low(q_ref[:, cols])
            q_wide = _augment(lane_q < HEAD_DIM, lane_q, HEAD_DIM, q_low, True,
                              [part[:, head:head + 1] for part in query_parts]).astype(BF16)
            q_cached = jnp.dot(q_low.astype(BF16), to_slot[cached_slot],
                               preferred_element_type=F32).astype(BF16)
            s_hist = (jnp.dot(q_cached, kt_pair, preferred_element_type=F32)
                      + lax.dot_general(q_wide, k_sc[0:past, :], nt, preferred_element_type=F32))
            s_new = lax.dot_general(q_wide, k_sc[past:past + lanes, :], nt, preferred_element_type=F32)
            s_new = jnp.where(causal, s_new, -jnp.inf)
            m = jnp.maximum(jnp.max(s_hist, axis=1, keepdims=True), jnp.max(s_new, axis=1, keepdims=True))
            p_hist = jnp.exp2(s_hist - m).astype(BF16)
            acc_new = jnp.dot(jnp.exp2(s_new - m).astype(BF16), v_sc[...], preferred_element_type=F32)
            o_hist = lax.dot_general(p_hist, vt_pair, nt, preferred_element_type=F32)
            if cached_slot:
                o_hist = pltpu.roll(o_hist, HEAD_DIM, 1)
            denom = jnp.sum(p_hist.astype(F32), axis=1, keepdims=True) + acc_new[:, HEAD_DIM:HEAD_DIM + 1]
            o_head = ((o_hist + acc_new) / denom).astype(BF16)
            o_tiles[pair] = o_tiles[pair] + jnp.dot(o_head, to_slot[slot], preferred_element_type=F32)
    for pair in range(N_PAIRS):
        o_ref[:, pair * lanes:(pair + 1) * lanes] = o_tiles[pair].astype(o_ref.dtype)


def _attention_hist(q, k_new, v_new, k_hist, v_hist, c):
    b, t, d = q.shape
    past = k_hist.shape[1]
    lanes = V7X_LANES
    assert t <= lanes and t % V7X_SUBLANES == 0 and past % lanes == 0
    k_t = jnp.transpose(k_hist, (0, 2, 3, 1))
    v_t = jnp.transpose(v_hist, (0, 2, 3, 1))
    new_spec = pl.BlockSpec((None, t, d), lambda i: (i, 0, 0))
    hist_spec = pl.BlockSpec((None, N_HEADS, HEAD_DIM, past), lambda i: (i, 0, 0, 0))
    vmem = (2 * 2 * past * d * 4 + 8 * t * d * 2 + 2 * (past + t) * lanes * 4
            + (past + 2 * lanes) * lanes * 2 + 4 * past * lanes * 2 + 16 * (past + t) * lanes * 4)
    return pl.pallas_call(
        functools.partial(_attn_hist_kernel, past=past),
        grid=(b,),
        in_specs=[new_spec, new_spec, new_spec, hist_spec, hist_spec,
                  pl.BlockSpec((None, past + t, N_HEADS), lambda i: (i, 0, 0))],
        out_specs=new_spec,
        out_shape=jax.ShapeDtypeStruct((b, t, d), BF16),
        scratch_shapes=[pltpu.VMEM((past + lanes, lanes), BF16), pltpu.VMEM((lanes, lanes), BF16)],
        compiler_params=_params(("arbitrary",), vmem),
        name="fox_attention_hist",
    )(q, k_new, v_new, k_t, v_t, c)


def _neg_expm1(y, exp_half_y):
    series = y * (-1.0 / 120.0) - 1.0 / 24.0
    for coeff in (-1.0 / 6.0, -0.5, -1.0):
        series = series * y + coeff
    return jnp.where(y > -0.0625, y * series, 1.0 - exp_half_y * exp_half_y)


def _scan_groups(a, b):
    rows, width = a.shape
    sub = V7X_SUBLANES
    a = a.reshape(rows // sub, sub, width)
    b = b.reshape(rows // sub, sub, width)
    row = lax.broadcasted_iota(jnp.int32, a.shape, 1)
    for shift in (1, 2, 4):
        keep = row >= shift
        a_prev = jnp.where(keep, pltpu.roll(a, shift, 1), 1.0)
        b_prev = jnp.where(keep, pltpu.roll(b, shift, 1), 0.0)
        b = a * b_prev + b
        a = a * a_prev
    return a.reshape(rows, width), b.reshape(rows, width)


def _rnn_tile(xr, yr, weight_refs, state_scratch):
    cw_ref, cb_ref, wg_ref, ba_ref, bx_ref, lam_ref = weight_refs
    ext_sc, a_sc, b_sc, h_sc, hc_sc = state_scratch
    sub = V7X_SUBLANES
    tt, d = xr.shape
    ext_sc[sub:sub + tt, :] = xr
    cw = cw_ref[...]
    xc = ext_sc[sub - 3:sub - 3 + tt, :] * cw[0:1, :]
    for w in range(1, CONV_W):
        xc = xc + ext_sc[sub - 3 + w:sub - 3 + w + tt, :] * cw[w:w + 1, :]
    xc = cb_ref[...] + xc
    ext_sc[0:sub, :] = ext_sc[tt:tt + sub, :]

    xcb = xc.astype(BF16)
    wb = V7X_MXU_DIM
    for j in range(d // wb):
        cols = slice(j * wb, (j + 1) * wb)
        xcj = xc[:, cols]
        g = jnp.dot(xcb[:, cols], wg_ref[j], preferred_element_type=F32)
        r = jax.nn.sigmoid(g[:, :wb] + ba_ref[:, cols])
        i = jax.nn.sigmoid(g[:, wb:] + bx_ref[:, cols])
        log_a = (-RG_C * r) * _softplus(-lam_ref[:, cols])
        a = jnp.exp(log_a)
        bx = jnp.sqrt(_neg_expm1(2.0 * log_a, a)) * (i * xcj)
        a_sc[:, cols], b_sc[:, cols] = _scan_groups(a, bx)

    hc = hc_sc[...]
    for grp in range(tt // sub):
        rows = slice(grp * sub, (grp + 1) * sub)
        h = a_sc[rows, :] * hc + b_sc[rows, :]
        h_sc[rows, :] = h
        hc = h[sub - 1:sub, :]
    hc_sc[...] = hc
    return h_sc[...] * _gelu_tanh(yr)


def _init_rnn_state(state_scratch, conv_rows, h0):
    ext_sc, _, _, _, hc_sc = state_scratch
    ext_sc[0:V7X_SUBLANES, :] = conv_rows
    hc_sc[...] = h0


def _rnn_scratch(tt, d):
    return [pltpu.VMEM((tt + V7X_SUBLANES, d), F32), pltpu.VMEM((tt, d), F32),
            pltpu.VMEM((tt, d), F32), pltpu.VMEM((tt, d), F32), pltpu.VMEM((1, d), F32)]


def _rglru_kernel(*refs, has_state):
    if has_state:
        xr_ref, yr_ref, ch_ref, h0_ref = refs[:4]
        refs = refs[4:]
    else:
        xr_ref, yr_ref = refs[:2]
        refs = refs[2:]
    weight_refs, (o_ref, hl_ref), state_scratch = refs[:6], refs[6:8], refs[8:]
    d = xr_ref.shape[1]

    @pl.when(pl.program_id(1) == 0)
    def _first_tile():
        if has_state:
            _init_rnn_state(state_scratch, ch_ref[...], h0_ref[...])
        else:
            _init_rnn_state(state_scratch, jnp.zeros((V7X_SUBLANES, d), F32), jnp.zeros((1, d), F32))

    out = _rnn_tile(xr_ref[...].astype(F32), yr_ref[...].astype(F32), weight_refs, state_scratch)
    o_ref[...] = out.astype(o_ref.dtype)
    hl_ref[...] = state_scratch[4][...]


def _rglru(xr, yr, conv_hist, h0, cw, cb, wg, ba, bx, lam):
    b, t, d = xr.shape
    tt = min(RNN_TIME_TILE, t)
    assert t % tt == 0 and tt % V7X_SUBLANES == 0
    has_state = conv_hist is not None
    tok_spec = pl.BlockSpec((None, tt, d), lambda i, j: (i, j, 0))
    in_specs = [tok_spec, tok_spec]
    args = [xr, yr]
    if has_state:
        in_specs += [pl.BlockSpec((None, V7X_SUBLANES, d), lambda i, j: (i, 0, 0)),
                     pl.BlockSpec((None, 1, d), lambda i, j: (i, 0, 0))]
        args += [conv_hist, h0]
    weights = [cw, cb, wg, ba, bx, lam]
    in_specs += [_const_spec(w.shape) for w in weights]
    args += weights
    vmem = 4 * tt * d * 2 * 2 + 2 * tt * d * 2 + 4 * (tt + 8) * d * 4 + 16 * tt * d * 4
    return pl.pallas_call(
        functools.partial(_rglru_kernel, has_state=has_state),
        grid=(b, t // tt),
        in_specs=in_specs,
        out_specs=(tok_spec, pl.BlockSpec((None, 1, d), lambda i, j: (i, 0, 0))),
        out_shape=(jax.ShapeDtypeStruct((b, t, d), BF16), jax.ShapeDtypeStruct((b, 1, d), F32)),
        scratch_shapes=_rnn_scratch(tt, d),
        compiler_params=_params(("arbitrary", "arbitrary"), vmem),
        name="rglru",
    )(*args)


def _gate_rnn_kernel(x_ref, g_ref, w_ref, wf_ref, bf_ref, cw_ref, cb_ref, wg_ref, ba_ref, bx_ref, lam_ref,
                     o_ref, ga_ref, gb_ref, lf_ref, tail_ref, hl_ref, *state_scratch, tiles_per_stream):
    tm, d = x_ref.shape

    @pl.when(pl.program_id(0) % tiles_per_stream == 0)
    def _first_tile():
        _init_rnn_state(state_scratch, jnp.zeros((V7X_SUBLANES, d), F32), jnp.zeros((1, d), F32))

    hn = _rmsnorm(x_ref[...], g_ref[...]).astype(BF16)
    xr = jnp.dot(hn, w_ref[0], preferred_element_type=F32)
    tail_ref[0] = xr[tm - V7X_SUBLANES:tm, :]
    yr = jnp.dot(hn, w_ref[1], preferred_element_type=F32)
    ga_ref[...] = jnp.dot(hn, w_ref[2], preferred_element_type=F32).astype(ga_ref.dtype)
    gb_ref[...] = jnp.dot(hn, w_ref[3], preferred_element_type=F32).astype(gb_ref.dtype)
    zf = jnp.dot(hn, wf_ref[...], preferred_element_type=F32)
    lf_ref[...] = _log_sigmoid(zf + bf_ref[...])[:, :N_HEADS]
    weight_refs = (cw_ref, cb_ref, wg_ref, ba_ref, bx_ref, lam_ref)
    o_ref[...] = _rnn_tile(xr, yr, weight_refs, state_scratch).astype(o_ref.dtype)
    hl_ref[...] = state_scratch[4][...]


def _gate_rnn_proj(x2d, g, w_gate, wf, bf, rnn_weights, seq_len):
    n, d = x2d.shape
    tm = min(TOKEN_TILE, n)
    assert n % tm == 0 and seq_len % tm == 0
    tiles_per_stream = seq_len // tm
    n_streams = n // seq_len
    row = lambda i: (i, 0)
    stream = lambda i: (i // tiles_per_stream, 0, 0)
    tok_bf16 = jax.ShapeDtypeStruct((n, d), BF16)
    tok_spec = pl.BlockSpec((tm, d), row)
    weights = [g, w_gate, wf, bf, *rnn_weights]
    vmem = (sum(w.size * w.dtype.itemsize for w in weights) + 2 * tm * d * 4 + 2 * 3 * tm * d * 2
            + 4 * (tm + 8) * d * 4 + 14 * tm * d * 4)
    return pl.pallas_call(
        functools.partial(_gate_rnn_kernel, tiles_per_stream=tiles_per_stream),
        grid=(n // tm,),
        in_specs=[tok_spec] + [_const_spec(w.shape) for w in weights],
        out_specs=(tok_spec, tok_spec, tok_spec, pl.BlockSpec((tm, N_HEADS), row),
                   pl.BlockSpec((1, V7X_SUBLANES, d), stream), pl.BlockSpec((None, 1, d), stream)),
        out_shape=(tok_bf16, tok_bf16, tok_bf16, jax.ShapeDtypeStruct((n, N_HEADS), F32),
                   jax.ShapeDtypeStruct((n_streams, V7X_SUBLANES, d), F32),
                   jax.ShapeDtypeStruct((n_streams, 1, d), F32)),
        scratch_shapes=_rnn_scratch(tm, d),
        compiler_params=_params(("arbitrary",), vmem),
        name="gate_rnn_proj",
    )(x2d, *weights)


def _out_mlp_kernel(x_ref, oa_ref, or_ref, ga_ref, gb_ref, wpa_ref, wpr_ref, wo_ref, g2_ref,
                    wup_ref, wdn_ref, gf_ref, y_ref):
    d = x_ref.shape[1]
    ya = jnp.dot(oa_ref[...], wpa_ref[...], preferred_element_type=F32)
    yb = jnp.dot(or_ref[...], wpr_ref[...], preferred_element_type=F32)
    merged = (jax.nn.sigmoid(ga_ref[...].astype(F32)) * ya
              + jax.nn.sigmoid(gb_ref[...].astype(F32)) * yb)
    x1 = x_ref[...] + jnp.dot(merged.astype(BF16), wo_ref[...], preferred_element_type=F32)
    h2 = _rmsnorm(x1, g2_ref[...]).astype(BF16)
    acc = x1
    for j in range(wup_ref.shape[1] // d):
        cols = slice(j * d, (j + 1) * d)
        u = jnp.maximum(jnp.dot(h2, wup_ref[:, cols], preferred_element_type=F32), 0.0)
        acc = acc + jnp.dot((u * u).astype(BF16), wdn_ref[cols, :], preferred_element_type=F32)
    y_ref[...] = _rmsnorm(acc, gf_ref[...])


def _out_mlp(x2d, oa, orn, ga, gb, wpa, wpr, wo, g2, wup, wdn, gf):
    n, d = x2d.shape
    tm = min(TOKEN_TILE, n)
    assert n % tm == 0
    tok_spec = pl.BlockSpec((tm, d), lambda i: (i, 0))
    weights = [wpa, wpr, wo, g2, wup, wdn, gf]
    vmem = (sum(w.size * w.dtype.itemsize for w in weights) + 2 * tm * d * (4 + 4 * 2 + 4)
            + 10 * tm * d * 4)
    return pl.pallas_call(
        _out_mlp_kernel,
        grid=(n // tm,),
        in_specs=[tok_spec] * 5 + [_const_spec(w.shape) for w in weights],
        out_specs=tok_spec,
        out_shape=jax.ShapeDtypeStruct((n, d), F32),
        compiler_params=_params(("arbitrary",), vmem),
        name="out_mlp",
    )(x2d, oa, orn, ga, gb, *weights)


def _prepare_weights(norm_mix_g, w_in, b_f, conv_w, conv_b, w_rg_a, b_rg_a, w_rg_x, b_rg_x, rg_lambda,
                     w_proj_attn, w_proj_rnn, w_out, norm_mlp_g, w_up, w_down, norm_final_g):
    d = w_in.shape[0]
    d_attn = N_HEADS * HEAD_DIM
    qkv_end = 3 * d_attn
    rest = qkv_end + N_HEADS
    split = np.array([_head_of(HEADS_SPLIT, p, s) * HEAD_DIM + f for p in range(N_PAIRS)
                      for s in range(HEADS_PER_TILE) for f in range(HEAD_DIM)])
    w_q = w_in[:, :d_attn] * (HEAD_DIM ** -0.5 * LOG2_E)
    w_qkv = jnp.stack([w_q, w_in[:, d_attn:2 * d_attn], w_in[:, 2 * d_attn:qkv_end]]).astype(BF16)
    w_gate = jnp.stack([w_in[:, rest + s * d:rest + (s + 1) * d] for s in range(4)]).astype(BF16)
    pad = V7X_LANES - N_HEADS
    wf = jnp.pad(w_in[:, qkv_end:rest], ((0, 0), (0, pad))).astype(BF16)
    bf = jnp.pad(b_f, (0, pad)).reshape(1, V7X_LANES)

    def block_diag(w):
        n_tiles = w.shape[0] // GATE_BLOCKS_PER_TILE
        w4 = w.reshape(n_tiles, GATE_BLOCKS_PER_TILE, RNN_BLOCK, RNN_BLOCK)
        eye = jnp.eye(GATE_BLOCKS_PER_TILE, dtype=w.dtype)
        return jnp.einsum('jmde,mn->jmdne', w4, eye).reshape(n_tiles, V7X_MXU_DIM, V7X_MXU_DIM)

    wg = jnp.concatenate([block_diag(w_rg_a), block_diag(w_rg_x)], axis=-1).astype(BF16)
    row = lambda v: v.reshape(1, -1)
    return dict(
        qkv_proj=(row(norm_mix_g), w_qkv), qkv_proj_split=(row(norm_mix_g), w_qkv[:, :, split]),
        gate_proj=(row(norm_mix_g), w_gate, wf, bf),
        rglru=(conv_w, row(conv_b), wg, row(b_rg_a), row(b_rg_x), row(rg_lambda)),
        proj_attn=w_proj_attn.astype(BF16), proj_attn_split=w_proj_attn[split, :].astype(BF16),
        out_mlp=(w_proj_rnn.astype(BF16), w_out.astype(BF16), row(norm_mlp_g),
                 w_up.astype(BF16), w_down.astype(BF16), row(norm_final_g)))


def _layer(x, k_hist, v_hist, logf_hist, conv_hist, h0, wts):
    b, t, d = x.shape
    x2d = x.reshape(b * t, d)
    first = k_hist is None
    q, kb, vb, k_state, v_state = _qkv_proj(x2d, *wts['qkv_proj' if first else 'qkv_proj_split'],
                                            seq_len=t, position_minor=first)
    as_seq = lambda a: a.reshape(b, t, d)
    if first:
        o_rnn, ga, gb, lf, tail, h_last = _gate_rnn_proj(x2d, *wts['gate_proj'], wts['rglru'], seq_len=t)
        lf_new = lf.reshape(b, t, N_HEADS)
        o_attn = _attention_first(as_seq(q), as_seq(kb), as_seq(vb), _cumsum(lf_new))
        new_k, new_v = (jnp.transpose(s, (0, 3, 1, 2)) for s in (k_state, v_state))
    else:
        xr, yr, ga, gb, lf, tail = _gate_proj(x2d, *wts['gate_proj'], seq_len=t)
        lf_new = lf.reshape(b, t, N_HEADS)
        c = _cumsum(jnp.concatenate([logf_hist, lf_new], axis=1))
        o_attn = _attention_hist(as_seq(q), as_seq(kb), as_seq(vb), k_hist, v_hist, c)
        new_k, new_v = (s.reshape(b, t, N_HEADS, HEAD_DIM) for s in (k_state, v_state))
        conv_hist = jnp.pad(conv_hist, ((0, 0), (V7X_SUBLANES - (CONV_W - 1), 0), (0, 0)))
        o_rnn, h_last = _rglru(as_seq(xr), as_seq(yr), conv_hist, h0.reshape(b, 1, d), *wts['rglru'])
    y = _out_mlp(x2d, o_attn.reshape(b * t, d), o_rnn.reshape(b * t, d), ga, gb,
                 wts['proj_attn' if first else 'proj_attn_split'], *wts['out_mlp'])
    new_conv = tail[:, V7X_SUBLANES - (CONV_W - 1):, :]
    return y.reshape(b, t, d), new_k, new_v, lf_new, new_conv, h_last.reshape(b, d)


def kernel(x_prompt, x_sample, cache_k, cache_v, cache_logf, state_conv, state_rglru, norm_mix_g, w_in, b_f, conv_w, conv_b, w_rg_a, b_rg_a, w_rg_x, b_rg_x, rg_lambda, w_proj_attn, w_proj_rnn, w_out, norm_mlp_g, w_up, w_down, norm_final_g):
    assert w_in.shape[0] == 1, "the final norm is fused into the layer: one layer only"
    wts = _prepare_weights(norm_mix_g[0], w_in[0], b_f[0], conv_w[0], conv_b[0], w_rg_a[0], b_rg_a[0],
                           w_rg_x[0], b_rg_x[0], rg_lambda[0], w_proj_attn[0], w_proj_rnn[0], w_out[0],
                           norm_mlp_g[0], w_up[0], w_down[0], norm_final_g)
    yp, kp, vp, lp, cp, hp = _layer(x_prompt, None, None, None, None, None, wts)
    ys, ks, vs, ls, cs, hs = _layer(x_sample, cache_k[0], cache_v[0], cache_logf[0], state_conv[0],
                                    state_rglru[0], wts)
    lead = lambda a: a[None]
    return (yp, ys, lead(kp), lead(vp), lead(lp), lead(cp), lead(hp),
            lead(ks), lead(vs), lead(ls), lead(cs), lead(hs))
```
